```python
import math
import jax, jax.numpy as jnp
from jax import lax
import numpy as np

D_MODEL = 4096
BATCH = 2
SEQ = 8192
DEPTH = 2

MIX_WIDTH = D_MODEL
GROUP_WIDTH = MIX_WIDTH // 4
CHUNK = 128
GMLP_HEAD_DIM = 128
GMLP_HEADS = GROUP_WIDTH // GMLP_HEAD_DIM
CONV_WIDTH = 31
CONV_PAD = CONV_WIDTH // 2
POOL_WINDOWS = (2, 4, 8, 16)
POOL_CH = GROUP_WIDTH // len(POOL_WINDOWS)
DIFF_HEAD_DIM = 64
DIFF_HEADS = GROUP_WIDTH // (2 * DIFF_HEAD_DIM)
ROT_DIM = DIFF_HEAD_DIM // 4
ROPE_THETA = 500000.0
Q_BLOCK = 128
D_FF = 4 * D_MODEL
IN_WIDTHS = (2 * GROUP_WIDTH, 2 * GROUP_WIDTH, GROUP_WIDTH, GROUP_WIDTH, GROUP_WIDTH, GROUP_WIDTH)
IN_SPLITS = tuple(int(s) for s in np.cumsum(IN_WIDTHS)[:-1])
D_IN = sum(IN_WIDTHS)

kernel_name = "hybrid_parallel_group_encoder_block"


def _rms_norm(x, g, eps=1e-6):
    xf = x.astype(jnp.float32)
    y = xf * lax.rsqrt(jnp.mean(xf * xf, axis=-1, keepdims=True) + eps)
    return (y * g.astype(jnp.float32)).astype(x.dtype)


def _layer_norm(x, g, b, eps=1e-5):
    xf = x.astype(jnp.float32)
    mu = jnp.mean(xf, axis=-1, keepdims=True)
    xc = xf - mu
    y = xc * lax.rsqrt(jnp.mean(xc * xc, axis=-1, keepdims=True) + eps)
    return (y * g.astype(jnp.float32) + b.astype(jnp.float32)).astype(x.dtype)


def _rope(x, cos, sin):
    half = ROT_DIM // 2
    shape = (1, x.shape[1]) + (1,) * (x.ndim - 3) + (half,)
    c = cos.reshape(shape)
    s = sin.reshape(shape)
    x1 = x[..., :half].astype(jnp.float32)
    x2 = x[..., half:ROT_DIM].astype(jnp.float32)
    return jnp.concatenate([(x1 * c - x2 * s).astype(x.dtype),
                            (x2 * c + x1 * s).astype(x.dtype),
                            x[..., ROT_DIM:]], axis=-1)


def _gmlp_mixer(z, ln_g, ln_b, ws, bs):
    z = jax.nn.gelu(z, approximate=False)
    u, v = jnp.split(z, 2, axis=-1)
    v = _layer_norm(v, ln_g, ln_b)
    b_, s_, _ = v.shape
    vc = v.reshape(b_, s_ // CHUNK, CHUNK, GMLP_HEADS, GMLP_HEAD_DIM)
    sv = jnp.einsum('hij,bnjhc->bnihc', ws, vc) + bs.T[None, None, :, :, None]
    return u * sv.reshape(b_, s_, GROUP_WIDTH)


def _conv_mixer(z, w, b, ln_g, ln_b):
    a, gate = jnp.split(z, 2, axis=-1)
    h = a * jax.nn.sigmoid(gate)
    h = lax.conv_general_dilated(h, w[:, None, :], window_strides=(1,),
                                 padding=[(CONV_PAD, CONV_PAD)],
                                 dimension_numbers=('NWC', 'WIO', 'NWC'),
                                 feature_group_count=GROUP_WIDTH) + b
    h = _layer_norm(h, ln_g, ln_b)
    return jax.nn.silu(h)


def _pool_mixer(z, pool_w, scale):
    b_, s_, _ = z.shape
    zf = z.astype(jnp.float32)
    cs = jnp.concatenate([jnp.zeros((b_, 1, GROUP_WIDTH), jnp.float32),
                          jnp.cumsum(zf, axis=1)], axis=1)
    t = jnp.arange(s_)
    outs = []
    for gi, w in enumerate(POOL_WINDOWS):
        lo = jnp.clip(t - w // 2, 0, s_)
        hi = jnp.clip(t + w // 2, 0, s_)
        sl = slice(gi * POOL_CH, (gi + 1) * POOL_CH)
        csg = cs[:, :, sl]
        mean = (csg[:, hi] - csg[:, lo]) / (hi - lo).astype(jnp.float32)[None, :, None]
        outs.append(mean - zf[:, :, sl])
    pooled = jnp.stack(outs, axis=2).astype(z.dtype)
    y = jnp.einsum('bsgc,gcd->bsgd', pooled, pool_w).reshape(b_, s_, GROUP_WIDTH)
    return y * scale


def _diff_attention(zq, zk, zv, qn_g, kn_g, lq1, lk1, lq2, lk2, subln_g, cos, sin, lam_init):
    b_, s_, _ = zq.shape
    q = zq.reshape(b_, s_, DIFF_HEADS, 2, DIFF_HEAD_DIM)
    k = zk.reshape(b_, s_, DIFF_HEADS, 2, DIFF_HEAD_DIM)
    vf = zv.reshape(b_, s_, DIFF_HEADS, 2 * DIFF_HEAD_DIM).astype(jnp.float32)
    q = _rope(_rms_norm(q, qn_g), cos, sin) * (DIFF_HEAD_DIM ** -0.5)
    k = _rope(_rms_norm(k, kn_g), cos, sin)
    f32 = jnp.float32
    lam = (jnp.exp(jnp.sum(lq1.astype(f32) * lk1.astype(f32)))
           - jnp.exp(jnp.sum(lq2.astype(f32) * lk2.astype(f32))) + lam_init)
    nb = s_ // Q_BLOCK
    qb = q.reshape(b_, nb, Q_BLOCK, DIFF_HEADS, 2, DIFF_HEAD_DIM).transpose(1, 0, 2, 3, 4, 5)

    def block(qblk):
        s = jnp.einsum('bqhcd,bkhcd->bhcqk', qblk, k).astype(f32)
        p = jax.nn.softmax(s, axis=-1)
        a = p[:, :, 0] - lam * p[:, :, 1]
        return jnp.einsum('bhqk,bkhe->bqhe', a, vf)

    o = lax.map(block, qb)
    o = o.transpose(1, 0, 2, 3, 4).reshape(b_, s_, DIFF_HEADS, 2 * DIFF_HEAD_DIM).astype(zq.dtype)
    o = _rms_norm(o, subln_g) * (1.0 - lam_init)
    return o.reshape(b_, s_, GROUP_WIDTH)


def setup_inputs(seed: int = 0) -> dict:
    key = jax.random.key(seed)
    ks = jax.random.split(key, 32)
    f32 = jnp.float32
    L, D, G = DEPTH, D_MODEL, GROUP_WIDTH

    def nrm(k, shape, scale):
        return jax.random.normal(k, shape, f32) * scale

    def gain(k, shape):
        return 1.0 + 0.01 * jax.random.normal(k, shape, f32)

    return {
        "x": nrm(ks[0], (BATCH, SEQ, D), 1.0),
        "attn_norm_g": gain(ks[1], (L, D)),
        "w_in": nrm(ks[2], (L, D, D_IN), D ** -0.5),
        "gmlp_ln_g": gain(ks[3], (L, G)),
        "gmlp_ln_b": nrm(ks[4], (L, G), 0.01),
        "gmlp_ws": nrm(ks[5], (L, GMLP_HEADS, CHUNK, CHUNK), CHUNK ** -0.5),
        "gmlp_bs": gain(ks[6], (L, GMLP_HEADS, CHUNK)),
        "conv_w": nrm(ks[7], (L, CONV_WIDTH, G), CONV_WIDTH ** -0.5),
        "conv_b": nrm(ks[8], (L, G), 0.01),
        "conv_ln_g": gain(ks[9], (L, G)),
        "conv_ln_b": nrm(ks[10], (L, G), 0.01),
        "pool_w": nrm(ks[11], (L, len(POOL_WINDOWS), POOL_CH, POOL_CH), POOL_CH ** -0.5),
        "pool_scale": 1.0 + 0.1 * jax.random.normal(ks[12], (L, G), f32),
        "q_norm_g": gain(ks[13], (L, DIFF_HEAD_DIM)),
        "k_norm_g": gain(ks[14], (L, DIFF_HEAD_DIM)),
        "lambda_q1": nrm(ks[15], (L, DIFF_HEAD_DIM), 0.1),
        "lambda_k1": nrm(ks[16], (L, DIFF_HEAD_DIM), 0.1),
        "lambda_q2": nrm(ks[17], (L, DIFF_HEAD_DIM), 0.1),
        "lambda_k2": nrm(ks[18], (L, DIFF_HEAD_DIM), 0.1),
        "subln_g": gain(ks[19], (L, 2 * DIFF_HEAD_DIM)),
        "w_out": nrm(ks[20], (L, MIX_WIDTH, D), MIX_WIDTH ** -0.5),
        "mlp_norm_g": gain(ks[21], (L, D)),
        "w_up": nrm(ks[22], (L, D, D_FF), D ** -0.5),
        "w_down": nrm(ks[23], (L, D_FF, D), D_FF ** -0.5),
    }


def reference(x, attn_norm_g, w_in, gmlp_ln_g, gmlp_ln_b, gmlp_ws, gmlp_bs, conv_w, conv_b,
              conv_ln_g, conv_ln_b, pool_w, pool_scale, q_norm_g, k_norm_g, lambda_q1,
              lambda_k1, lambda_q2, lambda_k2, subln_g, w_out, mlp_norm_g, w_up, w_down):
    s_ = x.shape[1]
    pos = jnp.arange(s_, dtype=jnp.float32)
    inv_freq = ROPE_THETA ** (-jnp.arange(0, ROT_DIM, 2, dtype=jnp.float32) / ROT_DIM)
    ang = pos[:, None] * inv_freq[None, :]
    cos, sin = jnp.cos(ang), jnp.sin(ang)
    for l in range(DEPTH):
        lam_init = 0.8 - 0.6 * math.exp(-0.3 * l)
        h = _rms_norm(x, attn_norm_g[l])
        z = h @ w_in[l]
        z_a, z_b, z_c, z_q, z_k, z_v = jnp.split(z, IN_SPLITS, axis=-1)
        y_a = _gmlp_mixer(z_a, gmlp_ln_g[l], gmlp_ln_b[l], gmlp_ws[l], gmlp_bs[l])
        y_b = _conv_mixer(z_b, conv_w[l], conv_b[l], conv_ln_g[l], conv_ln_b[l])
        y_c = _pool_mixer(z_c, pool_w[l], pool_scale[l])
        y_d = _diff_attention(z_q, z_k, z_v, q_norm_g[l], k_norm_g[l], lambda_q1[l],
                              lambda_k1[l], lambda_q2[l], lambda_k2[l], subln_g[l],
                              cos, sin, lam_init)
        y = jnp.concatenate([y_a, y_b, y_c, y_d], axis=-1) @ w_out[l]
        x = x + y
        h = _rms_norm(x, mlp_norm_g[l])
        x = x + jnp.square(jax.nn.relu(h @ w_up[l])) @ w_down[l]
    return x
```

```python
import functools
import math

import jax
import jax.numpy as jnp
import numpy as np
from jax import lax
from jax.experimental import pallas as pl
from jax.experimental.pallas import tpu as pltpu

F32 = jnp.float32
BF16 = jnp.bfloat16

CHUNK = 128
CONV_WIDTH = 31
CONV_PAD = CONV_WIDTH // 2
POOL_WINDOWS = (2, 4, 8, 16)
DIFF_HEAD_DIM = 64
HEAD_W = 2 * DIFF_HEAD_DIM
ROT_DIM = DIFF_HEAD_DIM // 4
ROPE_THETA = 500000.0
RMS_EPS = 1e-6
LN_EPS = 1e-5

HALO = 16
VMEM_LIMIT = 52 * 1024 * 1024


def _cparams(sem):
    return pltpu.CompilerParams(dimension_semantics=sem, vmem_limit_bytes=VMEM_LIMIT)


def _pick(n, pref):
    t = min(pref, n)
    while n % t:
        t //= 2
    return t


def _rms_rows(x, g):
    ms = jnp.mean(x * x, axis=-1, keepdims=True)
    return x * lax.rsqrt(ms + RMS_EPS) * g


def _norm_kernel(x_ref, g_ref, h_ref):
    h_ref[...] = _rms_rows(x_ref[...], g_ref[...]).astype(h_ref.dtype)


def _add_norm_kernel(x_ref, y_ref, g_ref, xo_ref, h_ref):
    x = x_ref[...] + y_ref[...]
    xo_ref[...] = x
    h_ref[...] = _rms_rows(x, g_ref[...]).astype(h_ref.dtype)


def _add_kernel(x_ref, y_ref, o_ref):
    o_ref[...] = x_ref[...] + y_ref[...]


def _rmsnorm(x, g, y=None):
    t, d = x.shape
    r = _pick(t, 128)
    row = pl.BlockSpec((r, d), lambda i: (i, 0))
    gs = pl.BlockSpec((1, d), lambda i: (0, 0))
    g2 = g.reshape(1, d)
    if y is None:
        return pl.pallas_call(
            _norm_kernel, grid=(t // r,), in_specs=[row, gs], out_specs=row,
            out_shape=jax.ShapeDtypeStruct((t, d), BF16),
            compiler_params=_cparams(("parallel",)), name="rmsnorm")(x, g2)
    return pl.pallas_call(
        _add_norm_kernel, grid=(t // r,), in_specs=[row, row, gs], out_specs=[row, row],
        out_shape=[jax.ShapeDtypeStruct((t, d), F32), jax.ShapeDtypeStruct((t, d), BF16)],
        compiler_params=_cparams(("parallel",)), name="add_rmsnorm")(x, y, g2)


def _add(x, y):
    t, d = x.shape
    r = _pick(t, 256)
    row = pl.BlockSpec((r, d), lambda i: (i, 0))
    return pl.pallas_call(
        _add_kernel, grid=(t // r,), in_specs=[row, row], out_specs=row,
        out_shape=jax.ShapeDtypeStruct((t, d), F32),
        compiler_params=_cparams(("parallel",)), name="residual_add")(x, y)


def _mm_kernel(a_ref, w_ref, o_ref, *, relu2):
    acc = jnp.dot(a_ref[...], w_ref[...], preferred_element_type=F32)
    if relu2:
        acc = jnp.square(jnp.maximum(acc, 0.0))
    o_ref[...] = acc.astype(o_ref.dtype)


def _matmul(a, w, out_dtype, *, relu2=False, name):
    m, k = a.shape
    n = w.shape[1]
    tm, tn = _pick(m, 1024), _pick(n, 1024)
    return pl.pallas_call(
        functools.partial(_mm_kernel, relu2=relu2), grid=(m // tm, n // tn),
        in_specs=[pl.BlockSpec((tm, k), lambda i, j: (i, 0)),
                  pl.BlockSpec((k, tn), lambda i, j: (0, j))],
        out_specs=pl.BlockSpec((tm, tn), lambda i, j: (i, j)),
        out_shape=jax.ShapeDtypeStruct((m, n), out_dtype),
        compiler_params=_cparams(("parallel", "parallel")), name=name)(a, w)


def _mm4_kernel(a0_ref, a1_ref, a2_ref, a3_ref, w_ref, o_ref, *, g):
    acc = jnp.dot(a0_ref[...], w_ref[0:g, :], preferred_element_type=F32)
    for idx, a_ref in enumerate((a1_ref, a2_ref, a3_ref), start=1):
        acc += jnp.dot(a_ref[...], w_ref[idx * g:(idx + 1) * g, :], preferred_element_type=F32)
    o_ref[...] = acc


def _out_proj(ys, w):
    m, g = ys[0].shape
    n = w.shape[1]
    tm, tn = _pick(m, 1024), _pick(n, 1024)
    a_spec = pl.BlockSpec((tm, g), lambda i, j: (i, 0))
    return pl.pallas_call(
        functools.partial(_mm4_kernel, g=g), grid=(m // tm, n // tn),
        in_specs=[a_spec, a_spec, a_spec, a_spec, pl.BlockSpec((4 * g, tn), lambda i, j: (0, j))],
        out_specs=pl.BlockSpec((tm, tn), lambda i, j: (i, j)),
        out_shape=jax.ShapeDtypeStruct((m, n), F32),
        compiler_params=_cparams(("parallel", "parallel")), name="out_proj")(*ys, w)


def _mmk_kernel(a_ref, w_ref, o_ref, acc_ref):
    kk = pl.program_id(2)

    @pl.when(kk == 0)
    def _():
        acc_ref[...] = jnp.zeros_like(acc_ref)

    acc_ref[...] += jnp.dot(a_ref[...], w_ref[...], preferred_element_type=F32)

    @pl.when(kk == pl.num_programs(2) - 1)
    def _():
        o_ref[...] = acc_ref[...]


def _matmul_ktiled(a, w, *, name):
    m, k = a.shape
    n = w.shape[1]
    tm, tn, tk = _pick(m, 1024), _pick(n, 1024), _pick(k, 2048)
    return pl.pallas_call(
        _mmk_kernel, grid=(m // tm, n // tn, k // tk),
        in_specs=[pl.BlockSpec((tm, tk), lambda i, j, kk: (i, kk)),
                  pl.BlockSpec((tk, tn), lambda i, j, kk: (kk, j))],
        out_specs=pl.BlockSpec((tm, tn), lambda i, j, kk: (i, j)),
        out_shape=jax.ShapeDtypeStruct((m, n), F32),
        scratch_shapes=[pltpu.VMEM((tm, tn), F32)],
        compiler_params=_cparams(("parallel", "parallel", "arbitrary")), name=name)(a, w)


def _layer_norm_rows(x, g, b):
    mu = jnp.mean(x, axis=-1, keepdims=True)
    xc = x - mu
    var = jnp.mean(xc * xc, axis=-1, keepdims=True)
    return xc * lax.rsqrt(var + LN_EPS) * g + b


def _gmlp_kernel(z_ref, lng_ref, lnb_ref, ws_ref, bsb_ref, o_ref, *, g, heads):
    r = z_ref.shape[0]
    z = z_ref[...]
    act = 0.5 * z * (1.0 + lax.erf(z * np.float32(math.sqrt(0.5))))
    u = act[:, :g]
    v = _layer_norm_rows(act[:, g:], lng_ref[...], lnb_ref[...]).astype(BF16)
    for c in range(r // CHUNK):
        rows = slice(c * CHUNK, (c + 1) * CHUNK)
        for h in range(heads):
            cols = slice(h * CHUNK, (h + 1) * CHUNK)
            sv = jnp.dot(ws_ref[h], v[rows, cols], preferred_element_type=F32) + bsb_ref[:, cols]
            o_ref[rows, cols] = (u[rows, cols] * sv).astype(o_ref.dtype)


def _gmlp_mixer(z, ln_g, ln_b, ws, bs, g):
    t = z.shape[0]
    heads = g // CHUNK
    r = _pick(t, 256)
    bsb = jnp.repeat(bs.T, CHUNK, axis=1)
    vec = pl.BlockSpec((1, g), lambda i: (0, 0))
    return pl.pallas_call(
        functools.partial(_gmlp_kernel, g=g, heads=heads), grid=(t // r,),
        in_specs=[pl.BlockSpec((r, 2 * g), lambda i: (i, 0)), vec, vec,
                  pl.BlockSpec((heads, CHUNK, CHUNK), lambda i: (0, 0, 0)),
                  pl.BlockSpec((CHUNK, g), lambda i: (0, 0))],
        out_specs=pl.BlockSpec((r, g), lambda i: (i, 0)),
        out_shape=jax.ShapeDtypeStruct((t, g), BF16),
        compiler_params=_cparams(("parallel",)), name="gmlp_mixer")(
            z, ln_g.reshape(1, g), ln_b.reshape(1, g), ws.astype(BF16), bsb)


def _halo_specs(r, width, col_block, seq, total):
    per_seq = seq // r
    hb = r // HALO
    last_halo_block = total // HALO - 1

    def main(b, i):
        return (b * per_seq + i, col_block)

    def prev(b, i):
        return (jnp.maximum((b * per_seq + i) * hb - 1, 0), col_block)

    def nxt(b, i):
        return (jnp.minimum((b * per_seq + i + 1) * hb, last_halo_block), col_block)

    return [pl.BlockSpec((r, width), main), pl.BlockSpec((HALO, width), prev),
            pl.BlockSpec((HALO, width), nxt)]


def _conv_kernel(z_ref, zp_ref, zn_ref, w_ref, b_ref, lng_ref, lnb_ref, o_ref, hbuf, *, g, rb):
    i = pl.program_id(1)
    r = z_ref.shape[0]

    def glu(zz):
        return zz[:, :g] * jax.nn.sigmoid(zz[:, g:])

    hbuf[0:HALO, :] = jnp.where(i > 0, glu(zp_ref[...]), 0.0)
    hbuf[HALO:HALO + r, :] = glu(z_ref[...])
    hbuf[HALO + r:, :] = jnp.where(i < pl.num_programs(1) - 1, glu(zn_ref[...]), 0.0)

    def group(q, carry):
        base = pl.multiple_of(q * rb, rb)
        slab = hbuf[pl.ds(base, rb + 2 * HALO), :]
        acc = jnp.zeros((rb, g), F32)
        for k in range(CONV_WIDTH):
            off = HALO - CONV_PAD + k
            acc = acc + slab[off:off + rb, :] * w_ref[k:k + 1, :]
        y = _layer_norm_rows(acc + b_ref[...], lng_ref[...], lnb_ref[...])
        o_ref[pl.ds(base, rb), :] = (y * jax.nn.sigmoid(y)).astype(o_ref.dtype)
        return carry

    lax.fori_loop(0, r // rb, group, 0)


def _conv_mixer(z, w, b, ln_g, ln_b, g, batch, seq):
    t = z.shape[0]
    r = _pick(seq, 256)
    rb = 8
    vec = pl.BlockSpec((1, g), lambda bb, i: (0, 0))
    return pl.pallas_call(
        functools.partial(_conv_kernel, g=g, rb=rb), grid=(batch, seq // r),
        in_specs=_halo_specs(r, 2 * g, 1, seq, t) + [
            pl.BlockSpec((CONV_WIDTH, g), lambda bb, i: (0, 0)), vec, vec, vec],
        out_specs=pl.BlockSpec((r, g), lambda bb, i: (bb * (seq // r) + i, 0)),
        out_shape=jax.ShapeDtypeStruct((t, g), BF16),
        scratch_shapes=[pltpu.VMEM((r + 2 * HALO, g), F32)],
        compiler_params=_cparams(("parallel", "arbitrary")), name="conv_mixer")(
            z, z, z, w, b.reshape(1, g), ln_g.reshape(1, g), ln_b.reshape(1, g))


def _pool_kernel(z_ref, zp_ref, zn_ref, pw_ref, sc_ref, o_ref, zbuf, pbuf, *, g, seq):
    i = pl.program_id(1)
    r = z_ref.shape[0]
    pc = g // len(POOL_WINDOWS)
    zbuf[0:HALO, :] = jnp.where(i > 0, zp_ref[...], 0.0)
    zbuf[HALO:HALO + r, :] = z_ref[...]
    zbuf[HALO + r:, :] = jnp.where(i < pl.num_programs(1) - 1, zn_ref[...], 0.0)
    pos = i * r + lax.broadcasted_iota(jnp.int32, (r, 1), 0)
    for gi, win in enumerate(POOL_WINDOWS):
        cols = slice(gi * pc, (gi + 1) * pc)
        half = win // 2
        tot = zbuf[HALO - half:HALO - half + r, cols]
        for d in range(1 - half, half):
            tot = tot + zbuf[HALO + d:HALO + d + r, cols]
        cnt = (jnp.minimum(pos + half, seq) - jnp.maximum(pos - half, 0)).astype(F32)
        pbuf[:, cols] = (tot / cnt - z_ref[:, cols]).astype(BF16)
    for gi in range(len(POOL_WINDOWS)):
        cols = slice(gi * pc, (gi + 1) * pc)
        y = jnp.dot(pbuf[:, cols], pw_ref[gi], preferred_element_type=F32)
        o_ref[:, cols] = (y * sc_ref[:, cols]).astype(o_ref.dtype)


def _pool_mixer(z, pool_w, scale, g, batch, seq):
    t = z.shape[0]
    r = _pick(seq, 256)
    pc = g // len(POOL_WINDOWS)
    return pl.pallas_call(
        functools.partial(_pool_kernel, g=g, seq=seq), grid=(batch, seq // r),
        in_specs=_halo_specs(r, g, 4, seq, t) + [
            pl.BlockSpec((len(POOL_WINDOWS), pc, pc), lambda bb, i: (0, 0, 0)),
            pl.BlockSpec((1, g), lambda bb, i: (0, 0))],
        out_specs=pl.BlockSpec((r, g), lambda bb, i: (bb * (seq // r) + i, 0)),
        out_shape=jax.ShapeDtypeStruct((t, g), BF16),
        scratch_shapes=[pltpu.VMEM((r + 2 * HALO, g), F32), pltpu.VMEM((r, g), BF16)],
        compiler_params=_cparams(("parallel", "arbitrary")), name="pool_mixer")(
            z, z, z, pool_w.astype(BF16), scale.reshape(1, g))


def _qk_prep_kernel(zq_ref, zk_ref, zv_ref, cos_ref, sin_ref, gq_ref, gk_ref, bd_ref,
                    q_ref, kt_ref, v_ref, *, heads):
    lane = lax.broadcasted_iota(jnp.int32, (1, HEAD_W), 1) % DIFF_HEAD_DIM
    first_half = lane < ROT_DIM // 2
    cos = cos_ref[...]
    sin = sin_ref[...]
    bd = bd_ref[...]

    def norm_rope(x, gain):
        x2 = x * x
        hi = x2.astype(BF16)
        lo = (x2 - hi.astype(F32)).astype(BF16)
        ss = (jnp.dot(hi, bd, preferred_element_type=F32)
              + jnp.dot(lo, bd, preferred_element_type=F32))
        xn = x * lax.rsqrt(ss * (1.0 / DIFF_HEAD_DIM) + RMS_EPS) * gain
        partner = jnp.where(first_half,
                            pltpu.roll(xn, HEAD_W - ROT_DIM // 2, axis=1),
                            pltpu.roll(xn, ROT_DIM // 2, axis=1))
        return xn * cos + partner * sin

    for h in range(heads):
        cols = slice(h * HEAD_W, (h + 1) * HEAD_W)
        q = norm_rope(zq_ref[:, cols], gq_ref[...]) * (DIFF_HEAD_DIM ** -0.5)
        q_ref[:, cols] = q.astype(q_ref.dtype)
        k = norm_rope(zk_ref[:, cols], gk_ref[...])
        kt_ref[0, h] = k.T.astype(kt_ref.dtype)
    v_ref[...] = zv_ref[...].astype(v_ref.dtype)


def _rope_tables(seq):
    pos = jnp.arange(seq, dtype=F32)
    inv_freq = ROPE_THETA ** (-jnp.arange(0, ROT_DIM, 2, dtype=F32) / ROT_DIM)
    ang = pos[:, None] * inv_freq[None, :]
    cos, sin = jnp.cos(ang), jnp.sin(ang)
    half = ROT_DIM // 2
    pad = DIFF_HEAD_DIM - ROT_DIM
    cos_c = jnp.concatenate([cos, cos, jnp.ones((seq, pad), F32)], axis=1)
    sin_c = jnp.concatenate([-sin, sin, jnp.zeros((seq, pad), F32)], axis=1)
    del half
    return jnp.tile(cos_c, (1, 2)), jnp.tile(sin_c, (1, 2))


def _qk_prep(z, qn_g, kn_g, cos_t, sin_t, g, batch, seq):
    t = z.shape[0]
    heads = g // HEAD_W
    r = _pick(seq, 256)
    per_seq = seq // r
    bd = jnp.kron(jnp.eye(2, dtype=F32), jnp.ones((DIFF_HEAD_DIM, DIFF_HEAD_DIM), F32)).astype(BF16)

    def zcol(c):
        return pl.BlockSpec((r, g), lambda i: (i, c))

    tab = pl.BlockSpec((r, HEAD_W), lambda i: (i % per_seq, 0))
    gain = pl.BlockSpec((1, HEAD_W), lambda i: (0, 0))
    row = pl.BlockSpec((r, g), lambda i: (i, 0))
    return pl.pallas_call(
        functools.partial(_qk_prep_kernel, heads=heads), grid=(t // r,),
        in_specs=[zcol(5), zcol(6), zcol(7), tab, tab, gain, gain,
                  pl.BlockSpec((HEAD_W, HEAD_W), lambda i: (0, 0))],
        out_specs=[row,
                   pl.BlockSpec((1, heads, HEAD_W, r), lambda i: (i // per_seq, 0, 0, i % per_seq)),
                   row],
        out_shape=[jax.ShapeDtypeStruct((t, g), BF16),
                   jax.ShapeDtypeStruct((batch, heads, HEAD_W, seq), BF16),
                   jax.ShapeDtypeStruct((t, g), BF16)],
        compiler_params=_cparams(("parallel",)), name="qk_prep")(
            z, z, z, cos_t, sin_t, jnp.tile(qn_g, 2).reshape(1, HEAD_W),
            jnp.tile(kn_g, 2).reshape(1, HEAD_W), bd)


def _attn_kernel(q_ref, kt_ref, v_ref, lam_ref, sg_ref, o_ref, *, tk, lam_init):
    tq = q_ref.shape[0]
    seq = v_ref.shape[0]
    d = DIFF_HEAD_DIM
    q = q_ref[...]
    qs = (q[:, :d], q[:, d:])

    def step(j, carry):
        start = pl.multiple_of(j * tk, tk)
        vj = v_ref[pl.ds(start, tk), :]
        out = []
        for c in range(2):
            m, l, acc = carry[c]
            s = jnp.dot(qs[c], kt_ref[0, 0, c * d:(c + 1) * d, pl.ds(start, tk)],
                        preferred_element_type=F32)
            m_new = jnp.maximum(m, jnp.max(s, axis=-1, keepdims=True))
            alpha = jnp.exp(m - m_new)
            p = jnp.exp(s - m_new)
            l = alpha * l + jnp.sum(p, axis=-1, keepdims=True)
            acc = alpha * acc + jnp.dot(p.astype(BF16), vj, preferred_element_type=F32)
            out.append((m_new, l, acc))
        return tuple(out)

    init = tuple((jnp.full((tq, 1), -jnp.inf, F32), jnp.zeros((tq, 1), F32),
                  jnp.zeros((tq, HEAD_W), F32)) for _ in range(2))
    (_, l0, a0), (_, l1, a1) = lax.fori_loop(0, seq // tk, step, init)

    lp = lam_ref[...]
    lam = (jnp.exp(jnp.sum(lp[0:1] * lp[1:2], axis=-1, keepdims=True))
           - jnp.exp(jnp.sum(lp[2:3] * lp[3:4], axis=-1, keepdims=True)) + lam_init)
    o = a0 / l0 - lam * (a1 / l1)
    o = _rms_rows(o, sg_ref[...]) * (1.0 - lam_init)
    o_ref[...] = o.astype(o_ref.dtype)


def _diff_attention(q, kt, v, lam_params, subln_g, lam_init, g, batch, seq):
    t = q.shape[0]
    heads = g // HEAD_W
    tq = _pick(seq, 256)
    tk = _pick(seq, 512)
    per_seq = seq // tq
    return pl.pallas_call(
        functools.partial(_attn_kernel, tk=tk, lam_init=lam_init),
        grid=(batch, heads, per_seq),
        in_specs=[pl.BlockSpec((tq, HEAD_W), lambda b, h, i: (b * per_seq + i, h)),
                  pl.BlockSpec((1, 1, HEAD_W, seq), lambda b, h, i: (b, h, 0, 0)),
                  pl.BlockSpec((seq, HEAD_W), lambda b, h, i: (b, h)),
                  pl.BlockSpec((4, DIFF_HEAD_DIM), lambda b, h, i: (0, 0)),
                  pl.BlockSpec((1, HEAD_W), lambda b, h, i: (0, 0))],
        out_specs=pl.BlockSpec((tq, HEAD_W), lambda b, h, i: (b * per_seq + i, h)),
        out_shape=jax.ShapeDtypeStruct((t, g), BF16),
        compiler_params=_cparams(("parallel", "parallel", "arbitrary")), name="diff_attention")(
            q, kt, v, lam_params, subln_g.reshape(1, HEAD_W))


def kernel(x, attn_norm_g, w_in, gmlp_ln_g, gmlp_ln_b, gmlp_ws, gmlp_bs, conv_w, conv_b, conv_ln_g, conv_ln_b, pool_w, pool_scale, q_norm_g, k_norm_g, lambda_q1, lambda_k1, lambda_q2, lambda_k2, subln_g, w_out, mlp_norm_g, w_up, w_down):
    batch, seq, d = x.shape
    depth = w_in.shape[0]
    g = d // 4
    t = batch * seq
    cos_t, sin_t = _rope_tables(seq)

    xr = x.reshape(t, d)
    pending = None
    for l in range(depth):
        lam_init = 0.8 - 0.6 * math.exp(-0.3 * l)
        if pending is None:
            h = _rmsnorm(xr, attn_norm_g[l])
        else:
            xr, h = _rmsnorm(xr, attn_norm_g[l], pending)
        z = _matmul(h, w_in[l].astype(BF16), F32, name="in_proj")
        y_a = _gmlp_mixer(z, gmlp_ln_g[l], gmlp_ln_b[l], gmlp_ws[l], gmlp_bs[l], g)
        y_b = _conv_mixer(z, conv_w[l], conv_b[l], conv_ln_g[l], conv_ln_b[l], g, batch, seq)
        y_c = _pool_mixer(z, pool_w[l], pool_scale[l], g, batch, seq)
        q, kt, v = _qk_prep(z, q_norm_g[l], k_norm_g[l], cos_t, sin_t, g, batch, seq)
        lam_params = jnp.stack([lambda_q1[l], lambda_k1[l], lambda_q2[l], lambda_k2[l]])
        y_d = _diff_attention(q, kt, v, lam_params, subln_g[l], lam_init, g, batch, seq)
        y = _out_proj((y_a, y_b, y_c, y_d), w_out[l].astype(BF16))
        xr, h = _rmsnorm(xr, mlp_norm_g[l], y)
        hid = _matmul(h, w_up[l].astype(BF16), BF16, relu2=True, name="mlp_up")
        pending = _matmul_ktiled(hid, w_down[l].astype(BF16), name="mlp_down")
    return _add(xr, pending).reshape(batch, seq, d)
```

```python
import functools
import math

import jax
import jax.numpy as jnp
import numpy as np
from jax import lax
from jax.experimental import pallas as pl
from jax.experimental.pallas import tpu as pltpu

F32 = jnp.float32
BF16 = jnp.bfloat16

CHUNK = 128
CONV_WIDTH = 31
CONV_PAD = CONV_WIDTH // 2
POOL_WINDOWS = (2, 4, 8, 16)
DIFF_HEAD_DIM = 64
HEAD_W = 2 * DIFF_HEAD_DIM
ROT_DIM = DIFF_HEAD_DIM // 4
ROPE_THETA = 500000.0
LOG2_E = math.log2(math.e)
RMS_EPS = 1e-6
LN_EPS = 1e-5

LANES = 128
SUBLANES = 8

ATTN_TQ = 512
ATTN_TK = 1024
CONV_ROWS = 64
HALO = 16
VMEM_LIMIT = 52 * 1024 * 1024


def _cparams(sem):
    return pltpu.CompilerParams(dimension_semantics=sem, vmem_limit_bytes=VMEM_LIMIT)


def _pick(n, pref):
    t = min(pref, n)
    while n % t:
        t //= 2
    return t


def _rms_rows(x, g):
    ms = jnp.mean(x * x, axis=-1, keepdims=True)
    return x * lax.rsqrt(ms + RMS_EPS) * g


def _norm_kernel(x_ref, g_ref, h_ref):
    h_ref[...] = _rms_rows(x_ref[...], g_ref[...]).astype(h_ref.dtype)


def _rmsnorm(x, g):
    t, d = x.shape
    r = _pick(t, 256)
    row = pl.BlockSpec((r, d), lambda i: (i, 0))
    return pl.pallas_call(
        _norm_kernel, grid=(t // r,),
        in_specs=[row, pl.BlockSpec((1, d), lambda i: (0, 0))], out_specs=row,
        out_shape=jax.ShapeDtypeStruct((t, d), BF16),
        compiler_params=_cparams(("parallel",)), name="rmsnorm")(x, g.reshape(1, d))


def _round_weight(w_ref, wb_ref, rows=512):
    k = wb_ref.shape[0]
    rows = _pick(k, rows)

    def chunk(c, carry):
        sl = pl.ds(pl.multiple_of(c * rows, rows), rows)
        wb_ref[sl, :] = w_ref[0, sl, :].astype(wb_ref.dtype)
        return carry

    lax.fori_loop(0, k // rows, chunk, 0)


def _mm_ws_kernel(a_ref, w_ref, o_ref, wb_ref, *, relu2):
    @pl.when(pl.program_id(1) == 0)
    def _():
        _round_weight(w_ref, wb_ref)

    acc = jnp.dot(a_ref[...], wb_ref[...], preferred_element_type=F32)
    if relu2:
        acc = jnp.square(jnp.maximum(acc, 0.0))
    o_ref[...] = acc.astype(o_ref.dtype)


def _matmul_ws(a, w_stack, layer, out_dtype, *, relu2=False, name):
    m, k = a.shape
    n = w_stack.shape[2]
    tm, tn = _pick(m, 1024), _pick(n, 512)
    return pl.pallas_call(
        functools.partial(_mm_ws_kernel, relu2=relu2), grid=(n // tn, m // tm),
        in_specs=[pl.BlockSpec((tm, k), lambda j, i: (i, 0)),
                  pl.BlockSpec((1, k, tn), lambda j, i: (layer, 0, j))],
        out_specs=pl.BlockSpec((tm, tn), lambda j, i: (i, j)),
        out_shape=jax.ShapeDtypeStruct((m, n), out_dtype),
        scratch_shapes=[pltpu.VMEM((k, tn), BF16)],
        compiler_params=_cparams(("parallel", "arbitrary")), name=name)(a, w_stack)


def _out_proj_kernel(a0_ref, a1_ref, a2_ref, a3_ref, w_ref, x_ref, o_ref, wb_ref, *, g):
    @pl.when(pl.program_id(1) == 0)
    def _():
        _round_weight(w_ref, wb_ref)

    acc = x_ref[...]
    for idx, a_ref in enumerate((a0_ref, a1_ref, a2_ref, a3_ref)):
        acc += jnp.dot(a_ref[...], wb_ref[idx * g:(idx + 1) * g, :], preferred_element_type=F32)
    o_ref[...] = acc


def _out_proj(ys, w_stack, layer, x):
    m, g = ys[0].shape
    n = w_stack.shape[2]
    tm, tn = _pick(m, 1024), _pick(n, 512)
    a_spec = pl.BlockSpec((tm, g), lambda j, i: (i, 0))
    tile = pl.BlockSpec((tm, tn), lambda j, i: (i, j))
    return pl.pallas_call(
        functools.partial(_out_proj_kernel, g=g), grid=(n // tn, m // tm),
        in_specs=[a_spec, a_spec, a_spec, a_spec,
                  pl.BlockSpec((1, 4 * g, tn), lambda j, i: (layer, 0, j)), tile],
        out_specs=tile,
        out_shape=jax.ShapeDtypeStruct((m, n), F32),
        scratch_shapes=[pltpu.VMEM((4 * g, tn), BF16)],
        compiler_params=_cparams(("parallel", "arbitrary")), name="out_proj")(*ys, w_stack, x)


def _mlp_down_kernel(a_ref, w_ref, x_ref, o_ref):
    @pl.when(pl.program_id(2) == 0)
    def _():
        o_ref[...] = x_ref[...]

    o_ref[...] += jnp.dot(a_ref[...], w_ref[0].astype(BF16), preferred_element_type=F32)


def _mlp_down(a, w_stack, layer, x):
    m, k = a.shape
    n = w_stack.shape[2]
    tm, tn, tk = _pick(m, 2048), _pick(n, 1024), _pick(k, 1024)
    return pl.pallas_call(
        _mlp_down_kernel, grid=(m // tm, n // tn, k // tk),
        in_specs=[pl.BlockSpec((tm, tk), lambda i, j, kk: (i, kk)),
                  pl.BlockSpec((1, tk, tn), lambda i, j, kk: (layer, kk, j)),
                  pl.BlockSpec((tm, tn), lambda i, j, kk: (i, j), pipeline_mode=pl.Buffered(1))],
        out_specs=pl.BlockSpec((tm, tn), lambda i, j, kk: (i, j)),
        out_shape=jax.ShapeDtypeStruct((m, n), F32),
        compiler_params=_cparams(("parallel", "parallel", "arbitrary")), name="mlp_down")(
            a, w_stack, x)


def _layer_norm_rows(x, g, b):
    mu = jnp.mean(x, axis=-1, keepdims=True)
    xc = x - mu
    var = jnp.mean(xc * xc, axis=-1, keepdims=True)
    return xc * lax.rsqrt(var + LN_EPS) * g + b


def _gmlp_kernel(z_ref, lng_ref, lnb_ref, ws_ref, bsb_ref, o_ref, *, g, heads):
    r = z_ref.shape[0]
    z = z_ref[...]
    act = 0.5 * z * (1.0 + lax.erf(z * np.float32(math.sqrt(0.5))))
    u = act[:, :g]
    v = _layer_norm_rows(act[:, g:], lng_ref[...], lnb_ref[...]).astype(BF16)
    for c in range(r // CHUNK):
        rows = slice(c * CHUNK, (c + 1) * CHUNK)
        for h in range(heads):
            cols = slice(h * CHUNK, (h + 1) * CHUNK)
            sv = jnp.dot(ws_ref[h], v[rows, cols], preferred_element_type=F32) + bsb_ref[:, cols]
            o_ref[rows, cols] = (u[rows, cols] * sv).astype(o_ref.dtype)


def _gmlp_mixer(z, ln_g, ln_b, ws, bs, g):
    t = z.shape[0]
    heads = g // CHUNK
    r = _pick(t, 256)
    bsb = jnp.repeat(bs.T, CHUNK, axis=1)
    vec = pl.BlockSpec((1, g), lambda i: (0, 0))
    return pl.pallas_call(
        functools.partial(_gmlp_kernel, g=g, heads=heads), grid=(t // r,),
        in_specs=[pl.BlockSpec((r, 2 * g), lambda i: (i, 0)), vec, vec,
                  pl.BlockSpec((heads, CHUNK, CHUNK), lambda i: (0, 0, 0)),
                  pl.BlockSpec((CHUNK, g), lambda i: (0, 0))],
        out_specs=pl.BlockSpec((r, g), lambda i: (i, 0)),
        out_shape=jax.ShapeDtypeStruct((t, g), BF16),
        compiler_params=_cparams(("parallel",)), name="gmlp_mixer")(
            z, ln_g.reshape(1, g), ln_b.reshape(1, g), ws.astype(BF16), bsb)


def _halo_specs(r, width, col_block, seq, total):
    per_seq = seq // r
    hb = r // HALO
    last_halo_block = total // HALO - 1

    def main(b, i):
        return (b * per_seq + i, col_block)

    def prev(b, i):
        return (jnp.maximum((b * per_seq + i) * hb - 1, 0), col_block)

    def nxt(b, i):
        return (jnp.minimum((b * per_seq + i + 1) * hb, last_halo_block), col_block)

    return [pl.BlockSpec((r, width), main), pl.BlockSpec((HALO, width), prev),
            pl.BlockSpec((HALO, width), nxt)]


def _conv_kernel(z_ref, zp_ref, zn_ref, w_ref, b_ref, lng_ref, lnb_ref, o_ref, hbuf, cbuf, *, g):
    i = pl.program_id(1)
    r = z_ref.shape[0]
    rb = cbuf.shape[0]
    slab_rows = rb + 2 * HALO

    def glu(zz):
        return zz[:, :g] * jax.nn.sigmoid(zz[:, g:])

    hbuf[0:HALO, :] = jnp.where(i > 0, glu(zp_ref[...]), 0.0)
    hbuf[HALO:HALO + r, :] = glu(z_ref[...])
    hbuf[HALO + r:, :] = jnp.where(i < pl.num_programs(1) - 1, glu(zn_ref[...]), 0.0)

    def group(q, carry):
        base = pl.multiple_of(q * rb, rb)
        for cb in range(g // LANES):
            cols = slice(cb * LANES, (cb + 1) * LANES)
            slab = hbuf[pl.ds(base, slab_rows), cols]
            acc = jnp.zeros((rb, LANES), F32)
            for b in range(SUBLANES):
                sh = slab if b == 0 else pltpu.roll(slab, slab_rows - b, axis=0)
                for a in range(slab_rows // SUBLANES):
                    k = SUBLANES * a + b - (HALO - CONV_PAD)
                    if 0 <= k < CONV_WIDTH:
                        acc = acc + sh[SUBLANES * a:SUBLANES * a + rb, :] * w_ref[k:k + 1, cols]
            cbuf[:, cols] = acc
        y = _layer_norm_rows(cbuf[...] + b_ref[...], lng_ref[...], lnb_ref[...])
        o_ref[pl.ds(base, rb), :] = (y * jax.nn.sigmoid(y)).astype(o_ref.dtype)
        return carry

    lax.fori_loop(0, r // rb, group, 0)


def _conv_mixer(z, w, b, ln_g, ln_b, g, batch, seq):
    t = z.shape[0]
    r = _pick(seq, 512)
    rb = _pick(r, CONV_ROWS)
    vec = pl.BlockSpec((1, g), lambda bb, i: (0, 0))
    return pl.pallas_call(
        functools.partial(_conv_kernel, g=g), grid=(batch, seq // r),
        in_specs=_halo_specs(r, 2 * g, 1, seq, t) + [
            pl.BlockSpec((CONV_WIDTH, g), lambda bb, i: (0, 0)), vec, vec, vec],
        out_specs=pl.BlockSpec((r, g), lambda bb, i: (bb * (seq // r) + i, 0)),
        out_shape=jax.ShapeDtypeStruct((t, g), BF16),
        scratch_shapes=[pltpu.VMEM((r + 2 * HALO, g), F32), pltpu.VMEM((rb, g), F32)],
        compiler_params=_cparams(("parallel", "arbitrary")), name="conv_mixer")(
            z, z, z, w, b.reshape(1, g), ln_g.reshape(1, g), ln_b.reshape(1, g))


def _pool_kernel(z_ref, zp_ref, zn_ref, pw_ref, sc_ref, o_ref, zbuf, pbuf, *, g, seq):
    i = pl.program_id(1)
    r = z_ref.shape[0]
    pc = g // len(POOL_WINDOWS)
    zbuf[0:HALO, :] = jnp.where(i > 0, zp_ref[...], 0.0)
    zbuf[HALO:HALO + r, :] = z_ref[...]
    zbuf[HALO + r:, :] = jnp.where(i < pl.num_programs(1) - 1, zn_ref[...], 0.0)
    pos = i * r + lax.broadcasted_iota(jnp.int32, (r, 1), 0)
    for gi, win in enumerate(POOL_WINDOWS):
        cols = slice(gi * pc, (gi + 1) * pc)
        half = win // 2
        tot = zbuf[HALO - half:HALO - half + r, cols]
        for d in range(1 - half, half):
            tot = tot + zbuf[HALO + d:HALO + d + r, cols]
        cnt = (jnp.minimum(pos + half, seq) - jnp.maximum(pos - half, 0)).astype(F32)
        pbuf[:, cols] = (tot / cnt - z_ref[:, cols]).astype(BF16)
    for gi in range(len(POOL_WINDOWS)):
        cols = slice(gi * pc, (gi + 1) * pc)
        y = jnp.dot(pbuf[:, cols], pw_ref[gi], preferred_element_type=F32)
        o_ref[:, cols] = (y * sc_ref[:, cols]).astype(o_ref.dtype)


def _pool_mixer(z, pool_w, scale, g, batch, seq):
    t = z.shape[0]
    r = _pick(seq, 256)
    pc = g // len(POOL_WINDOWS)
    return pl.pallas_call(
        functools.partial(_pool_kernel, g=g, seq=seq), grid=(batch, seq // r),
        in_specs=_halo_specs(r, g, 4, seq, t) + [
            pl.BlockSpec((len(POOL_WINDOWS), pc, pc), lambda bb, i: (0, 0, 0)),
            pl.BlockSpec((1, g), lambda bb, i: (0, 0))],
        out_specs=pl.BlockSpec((r, g), lambda bb, i: (bb * (seq // r) + i, 0)),
        out_shape=jax.ShapeDtypeStruct((t, g), BF16),
        scratch_shapes=[pltpu.VMEM((r + 2 * HALO, g), F32), pltpu.VMEM((r, g), BF16)],
        compiler_params=_cparams(("parallel", "arbitrary")), name="pool_mixer")(
            z, z, z, pool_w.astype(BF16), scale.reshape(1, g))


def _qk_prep_kernel(zq_ref, zk_ref, zv_ref, cos_ref, sin_ref, gq_ref, gk_ref, bd_ref,
                    q_ref, kt_ref, v_ref, *, heads):
    lane = lax.broadcasted_iota(jnp.int32, (1, HEAD_W), 1) % DIFF_HEAD_DIM
    first_half = lane < ROT_DIM // 2
    cos = cos_ref[...]
    sin = sin_ref[...]
    bd = bd_ref[...]

    def norm_rope(x, gain):
        x2 = x * x
        hi = x2.astype(BF16)
        lo = (x2 - hi.astype(F32)).astype(BF16)
        ss = (jnp.dot(hi, bd, preferred_element_type=F32)
              + jnp.dot(lo, bd, preferred_element_type=F32))
        xn = x * lax.rsqrt(ss * (1.0 / DIFF_HEAD_DIM) + RMS_EPS) * gain
        partner = jnp.where(first_half,
                            pltpu.roll(xn, HEAD_W - ROT_DIM // 2, axis=1),
                            pltpu.roll(xn, ROT_DIM // 2, axis=1))
        return xn * cos + partner * sin

    for h in range(heads):
        cols = slice(h * HEAD_W, (h + 1) * HEAD_W)
        q = norm_rope(zq_ref[:, cols], gq_ref[...]) * (DIFF_HEAD_DIM ** -0.5 * LOG2_E)
        q_ref[:, cols] = q.astype(q_ref.dtype)
        k = norm_rope(zk_ref[:, cols], gk_ref[...])
        kt_ref[0, h] = k.T.astype(kt_ref.dtype)
    v_ref[...] = zv_ref[...].astype(v_ref.dtype)


def _rope_tables(seq):
    pos = jnp.arange(seq, dtype=F32)
    inv_freq = ROPE_THETA ** (-jnp.arange(0, ROT_DIM, 2, dtype=F32) / ROT_DIM)
    ang = pos[:, None] * inv_freq[None, :]
    cos, sin = jnp.cos(ang), jnp.sin(ang)
    pad = DIFF_HEAD_DIM - ROT_DIM
    cos_c = jnp.concatenate([cos, cos, jnp.ones((seq, pad), F32)], axis=1)
    sin_c = jnp.concatenate([-sin, sin, jnp.zeros((seq, pad), F32)], axis=1)
    return jnp.tile(cos_c, (1, 2)), jnp.tile(sin_c, (1, 2))


def _qk_prep(z, qn_g, kn_g, cos_t, sin_t, g, batch, seq):
    t = z.shape[0]
    heads = g // HEAD_W
    r = _pick(seq, 256)
    per_seq = seq // r
    bd = jnp.kron(jnp.eye(2, dtype=F32), jnp.ones((DIFF_HEAD_DIM, DIFF_HEAD_DIM), F32)).astype(BF16)

    def zcol(c):
        return pl.BlockSpec((r, g), lambda i: (i, c))

    tab = pl.BlockSpec((r, HEAD_W), lambda i: (i % per_seq, 0))
    gain = pl.BlockSpec((1, HEAD_W), lambda i: (0, 0))
    row = pl.BlockSpec((r, g), lambda i: (i, 0))
    return pl.pallas_call(
        functools.partial(_qk_prep_kernel, heads=heads), grid=(t // r,),
        in_specs=[zcol(5), zcol(6), zcol(7), tab, tab, gain, gain,
                  pl.BlockSpec((HEAD_W, HEAD_W), lambda i: (0, 0))],
        out_specs=[row,
                   pl.BlockSpec((1, heads, HEAD_W, r), lambda i: (i // per_seq, 0, 0, i % per_seq)),
                   row],
        out_shape=[jax.ShapeDtypeStruct((t, g), BF16),
                   jax.ShapeDtypeStruct((batch, heads, HEAD_W, seq), BF16),
                   jax.ShapeDtypeStruct((t, g), BF16)],
        compiler_params=_cparams(("parallel",)), name="qk_prep")(
            z, z, z, cos_t, sin_t, jnp.tile(qn_g, 2).reshape(1, HEAD_W),
            jnp.tile(kn_g, 2).reshape(1, HEAD_W), bd)


def _attn_kernel(q_ref, kt_ref, v_ref, lam_ref, sg_ref, o_ref,
                 vext, s_a, s_b, mx_a, mx_b, m_run, acc, *, tq, tk, lam_init):
    seq = v_ref.shape[0]
    d = DIFF_HEAD_DIM
    nk = seq // tk
    half_nk = nk // 2
    n_iter = (seq // tq) * half_nk

    vext[:, :HEAD_W] = v_ref[...]
    vext[:, HEAD_W:] = jnp.ones((seq, HEAD_W), vext.dtype)
    m_run[...] = jnp.zeros_like(m_run)
    acc[...] = jnp.zeros_like(acc)

    lp = lam_ref[...]
    lam = (jnp.exp(jnp.sum(lp[0:1] * lp[1:2], axis=-1, keepdims=True))
           - jnp.exp(jnp.sum(lp[2:3] * lp[3:4], axis=-1, keepdims=True)) + lam_init)

    def scores(i, j, s_buf, mx_buf):
        qt = q_ref[pl.ds(pl.multiple_of(i * tq, tq), tq), :]
        keys = pl.ds(pl.multiple_of(j * tk, tk), tk)
        for c in range(2):
            s = jnp.dot(qt[:, c * d:(c + 1) * d], kt_ref[0, 0, c * d:(c + 1) * d, keys],
                        preferred_element_type=F32)
            s_buf[c] = s
            mx_buf[c] = jnp.broadcast_to(jnp.max(s, axis=-1, keepdims=True), (tq, HEAD_W))

    def accumulate(j, s_buf, mx_buf, first):
        vj = vext[pl.ds(pl.multiple_of(j * tk, tk), tk), :]
        for c in range(2):
            m_new = mx_buf[c]
            if first is None:
                m_new = jnp.maximum(m_run[c], m_new)
                alpha = jnp.exp2(m_run[c] - m_new)
                prev = acc[c] * jnp.concatenate([alpha, alpha], axis=1)
            else:
                m_old = jnp.where(first, -jnp.inf, m_run[c])
                m_new = jnp.maximum(m_old, m_new)
                alpha = jnp.exp2(m_old - m_new)
                prev = jnp.where(first, 0.0, acc[c] * jnp.concatenate([alpha, alpha], axis=1))
            m_run[c] = m_new
            p = jnp.concatenate(
                [jnp.exp2(s_buf[c, :, kb * HEAD_W:(kb + 1) * HEAD_W] - m_new).astype(BF16)
                 for kb in range(tk // HEAD_W)], axis=1)
            acc[c] = prev + jnp.dot(p, vj, preferred_element_type=F32)

    scores(0, 0, s_a, mx_a)

    def body(it, carry):
        i = it // half_nk
        j0 = 2 * (it % half_nk)
        nxt = jnp.minimum(it + 1, n_iter - 1)
        scores(i, j0 + 1, s_b, mx_b)
        accumulate(j0, s_a, mx_a, j0 == 0)
        scores(nxt // half_nk, 2 * (nxt % half_nk), s_a, mx_a)
        accumulate(j0 + 1, s_b, mx_b, None)

        @pl.when(j0 + 2 == nk)
        def _():
            a0 = acc[0]
            a1 = acc[1]
            o = a0[:, :HEAD_W] / a0[:, HEAD_W:] - lam * (a1[:, :HEAD_W] / a1[:, HEAD_W:])
            o = _rms_rows(o, sg_ref[...]) * (1.0 - lam_init)
            o_ref[pl.ds(pl.multiple_of(i * tq, tq), tq), :] = o.astype(o_ref.dtype)

        return carry

    lax.fori_loop(0, n_iter, body, 0)


def _diff_attention(q, kt, v, lam_params, subln_g, lam_init, g, batch, seq):
    t = q.shape[0]
    heads = g // HEAD_W
    tq = _pick(seq, ATTN_TQ)
    tk = _pick(seq // 2, ATTN_TK)
    return pl.pallas_call(
        functools.partial(_attn_kernel, tq=tq, tk=tk, lam_init=lam_init),
        grid=(batch, heads),
        in_specs=[pl.BlockSpec((seq, HEAD_W), lambda b, h: (b, h)),
                  pl.BlockSpec((1, 1, HEAD_W, seq), lambda b, h: (b, h, 0, 0)),
                  pl.BlockSpec((seq, HEAD_W), lambda b, h: (b, h)),
                  pl.BlockSpec((4, DIFF_HEAD_DIM), lambda b, h: (0, 0)),
                  pl.BlockSpec((1, HEAD_W), lambda b, h: (0, 0))],
        out_specs=pl.BlockSpec((seq, HEAD_W), lambda b, h: (b, h)),
        out_shape=jax.ShapeDtypeStruct((t, g), BF16),
        scratch_shapes=[pltpu.VMEM((seq, 2 * HEAD_W), BF16),
                        pltpu.VMEM((2, tq, tk), F32), pltpu.VMEM((2, tq, tk), F32),
                        pltpu.VMEM((2, tq, HEAD_W), F32), pltpu.VMEM((2, tq, HEAD_W), F32),
                        pltpu.VMEM((2, tq, HEAD_W), F32), pltpu.VMEM((2, tq, 2 * HEAD_W), F32)],
        compiler_params=_cparams(("parallel", "parallel")), name="diff_attention")(
            q, kt, v, lam_params, subln_g.reshape(1, HEAD_W))


def kernel(x, attn_norm_g, w_in, gmlp_ln_g, gmlp_ln_b, gmlp_ws, gmlp_bs, conv_w, conv_b, conv_ln_g, conv_ln_b, pool_w, pool_scale, q_norm_g, k_norm_g, lambda_q1, lambda_k1, lambda_q2, lambda_k2, subln_g, w_out, mlp_norm_g, w_up, w_down):
    batch, seq, d = x.shape
    depth = w_in.shape[0]
    g = d // 4
    t = batch * seq
    cos_t, sin_t = _rope_tables(seq)

    xr = x.reshape(t, d)
    for l in range(depth):
        lam_init = 0.8 - 0.6 * math.exp(-0.3 * l)
        h = _rmsnorm(xr, attn_norm_g[l])
        z = _matmul_ws(h, w_in, l, F32, name="in_proj")
        y_a = _gmlp_mixer(z, gmlp_ln_g[l], gmlp_ln_b[l], gmlp_ws[l], gmlp_bs[l], g)
        y_b = _conv_mixer(z, conv_w[l], conv_b[l], conv_ln_g[l], conv_ln_b[l], g, batch, seq)
        y_c = _pool_mixer(z, pool_w[l], pool_scale[l], g, batch, seq)
        q, kt, v = _qk_prep(z, q_norm_g[l], k_norm_g[l], cos_t, sin_t, g, batch, seq)
        lam_params = jnp.stack([lambda_q1[l], lambda_k1[l], lambda_q2[l], lambda_k2[l]])
        y_d = _diff_attention(q, kt, v, lam_params, subln_g[l], lam_init, g, batch, seq)
        xr = _out_proj((y_a, y_b, y_c, y_d), w_out, l, xr)
        h = _rmsnorm(xr, mlp_norm_g[l])
        hid = _matmul_ws(h, w_up, l, BF16, relu2=True, name="mlp_up")
        xr = _mlp_down(hid, w_down, l, xr)
    return xr.reshape(batch, seq, d)
```

```python
import functools
import math

import jax
import jax.numpy as jnp
import numpy as np
from jax import lax
from jax.experimental import pallas as pl
from jax.experimental.pallas import tpu as pltpu

F32 = jnp.float32
BF16 = jnp.bfloat16

CHUNK = 128
CONV_WIDTH = 31
CONV_PAD = CONV_WIDTH // 2
POOL_WINDOWS = (2, 4, 8, 16)
DIFF_HEAD_DIM = 64
HEAD_W = 2 * DIFF_HEAD_DIM
ROT_DIM = DIFF_HEAD_DIM // 4
ROPE_THETA = 500000.0
LOG2_E = math.log2(math.e)
RMS_EPS = 1e-6
LN_EPS = 1e-5

LANES = 128
SUBLANES = 8

ATTN_TQ = 512
ATTN_TK = 2048
CONV_ROWS = 64
HALO = 16
VMEM_LIMIT = 58 * 1024 * 1024


def _cparams(sem):
    return pltpu.CompilerParams(dimension_semantics=sem, vmem_limit_bytes=VMEM_LIMIT)


def _pick(n, pref):
    t = min(pref, n)
    while n % t:
        t //= 2
    return t


def _rms_rows(x, g):
    ms = jnp.mean(x * x, axis=-1, keepdims=True)
    return x * lax.rsqrt(ms + RMS_EPS) * g


def _lane_partial_sumsq(x):
    x2 = x * x
    part = x2[:, 0:LANES]
    for c in range(1, x.shape[1] // LANES):
        part = part + x2[:, c * LANES:(c + 1) * LANES]
    return part


def _row_rms_scale(ss_ref, inv_d):
    return lax.rsqrt(jnp.sum(ss_ref[...], axis=-1, keepdims=True) * inv_d + RMS_EPS)


class _Rounder:
    def __init__(self, w_stack, layer, grid, step_of):
        _, k, n = w_stack.shape
        steps = math.prod(grid)
        rows = max(2 * SUBLANES, k // steps)
        row_blocks = k // rows
        col_splits = steps // row_blocks
        cols = n // col_splits
        assert rows * row_blocks == k and row_blocks * col_splits == steps, (k, n, steps)
        assert cols * col_splits == n and cols % LANES == 0, (k, n, steps)

        def block_of(*ids):
            s = step_of(*ids)
            return s // col_splits, s % col_splits

        self.in_spec = pl.BlockSpec((1, rows, cols), lambda *ids: (layer, *block_of(*ids)))
        self.out_spec = pl.BlockSpec((rows, cols), lambda *ids: block_of(*ids))
        self.out_shape = jax.ShapeDtypeStruct((k, n), BF16)

    @staticmethod
    def run(src_ref, dst_ref):
        dst_ref[...] = src_ref[0].astype(dst_ref.dtype)


def _norm0_kernel(x_ref, g_ref, wsrc_ref, xg_ref, ss_ref, wdst_ref):
    x = x_ref[...]
    xg_ref[...] = (x * g_ref[...]).astype(xg_ref.dtype)
    ss_ref[...] = _lane_partial_sumsq(x)
    _Rounder.run(wsrc_ref, wdst_ref)


def _norm0(x, g, w_stack):
    t, d = x.shape
    r = _pick(t, 256)
    grid = (t // r,)
    rnd = _Rounder(w_stack, 0, grid, lambda i: i)
    row = pl.BlockSpec((r, d), lambda i: (i, 0))
    return pl.pallas_call(
        _norm0_kernel, grid=grid,
        in_specs=[row, pl.BlockSpec((1, d), lambda i: (0, 0)), rnd.in_spec],
        out_specs=[row, pl.BlockSpec((r, LANES), lambda i: (i, 0)), rnd.out_spec],
        out_shape=[jax.ShapeDtypeStruct((t, d), BF16), jax.ShapeDtypeStruct((t, LANES), F32),
                   rnd.out_shape],
        compiler_params=_cparams(("parallel",)), name="norm0")(x, g.reshape(1, d), w_stack)


def _scaled_mm_kernel(a_ref, w_ref, ss_ref, wsrc_ref, o_ref, wdst_ref, *, relu2, inv_d):
    acc = jnp.dot(a_ref[...], w_ref[...], preferred_element_type=F32) * _row_rms_scale(ss_ref, inv_d)
    if relu2:
        acc = jnp.square(jnp.maximum(acc, 0.0))
    o_ref[...] = acc.astype(o_ref.dtype)
    _Rounder.run(wsrc_ref, wdst_ref)


def _scaled_matmul(xg, ss, wb, out_dtype, round_stack, round_layer, *, relu2=False, name):
    m, k = xg.shape
    n = wb.shape[1]
    tm, tn = _pick(m, 1024), _pick(n, 1024)
    grid = (m // tm, n // tn)
    rnd = _Rounder(round_stack, round_layer, grid, lambda i, j: i * grid[1] + j)
    return pl.pallas_call(
        functools.partial(_scaled_mm_kernel, relu2=relu2, inv_d=1.0 / k), grid=grid,
        in_specs=[pl.BlockSpec((tm, k), lambda i, j: (i, 0)),
                  pl.BlockSpec((k, tn), lambda i, j: (0, j)),
                  pl.BlockSpec((tm, LANES), lambda i, j: (i, 0)), rnd.in_spec],
        out_specs=[pl.BlockSpec((tm, tn), lambda i, j: (i, j)), rnd.out_spec],
        out_shape=[jax.ShapeDtypeStruct((m, n), out_dtype), rnd.out_shape],
        compiler_params=_cparams(("parallel", "parallel")), name=name)(xg, wb, ss, round_stack)


def _emit_norm_inputs(xn, g_ref, xg_ref, ss_ref, first):
    xg_ref[...] = (xn * g_ref[...]).astype(xg_ref.dtype)
    part = _lane_partial_sumsq(xn)

    @pl.when(first)
    def _():
        ss_ref[...] = part

    @pl.when(jnp.logical_not(first))
    def _():
        ss_ref[...] += part


def _out_proj_kernel(a0_ref, a1_ref, a2_ref, a3_ref, w_ref, x_ref, g_ref, o_ref, xg_ref, ss_ref,
                     *, g):
    acc = x_ref[...]
    for idx, a_ref in enumerate((a0_ref, a1_ref, a2_ref, a3_ref)):
        acc += jnp.dot(a_ref[...], w_ref[idx * g:(idx + 1) * g, :], preferred_element_type=F32)
    o_ref[...] = acc
    _emit_norm_inputs(acc, g_ref, xg_ref, ss_ref, pl.program_id(1) == 0)


def _out_proj(ys, wb, x, gain):
    m, g = ys[0].shape
    n = wb.shape[1]
    tm, tn = _pick(m, 512), _pick(n, 1024)
    a_spec = pl.BlockSpec((tm, g), lambda i, j: (i, 0))
    tile = pl.BlockSpec((tm, tn), lambda i, j: (i, j))
    return pl.pallas_call(
        functools.partial(_out_proj_kernel, g=g), grid=(m // tm, n // tn),
        in_specs=[a_spec, a_spec, a_spec, a_spec, pl.BlockSpec((4 * g, tn), lambda i, j: (0, j)),
                  tile, pl.BlockSpec((1, tn), lambda i, j: (0, j))],
        out_specs=[tile, tile, pl.BlockSpec((tm, LANES), lambda i, j: (i, 0))],
        out_shape=[jax.ShapeDtypeStruct((m, n), F32), jax.ShapeDtypeStruct((m, n), BF16),
                   jax.ShapeDtypeStruct((m, LANES), F32)],
        compiler_params=_cparams(("parallel", "arbitrary")), name="out_proj")(
            *ys, wb, x, gain.reshape(1, n))


def _mlp_down_kernel(a_ref, w_ref, x_ref, *rest, emit_next):
    kk = pl.program_id(2)
    o_ref = rest[2] if emit_next else rest[0]

    @pl.when(kk == 0)
    def _():
        o_ref[...] = x_ref[...]

    o_ref[...] += jnp.dot(a_ref[...], w_ref[...], preferred_element_type=F32)

    if emit_next:
        g_ref, wsrc_ref, _, xg_ref, ss_ref, wdst_ref = rest
        first_col = pl.program_id(1) == 0

        @pl.when(kk == pl.num_programs(2) - 1)
        def _():
            _emit_norm_inputs(o_ref[...], g_ref, xg_ref, ss_ref, first_col)

        _Rounder.run(wsrc_ref, wdst_ref)


def _mlp_down(a, wb, x, next_gain=None, next_stack=None, next_layer=None):
    m, k = a.shape
    n = wb.shape[1]
    tm, tn, tk = _pick(m, 1024), _pick(n, 1024), _pick(k, 2048)
    grid = (m // tm, n // tn, k // tk)
    emit_next = next_gain is not None
    tile = pl.BlockSpec((tm, tn), lambda i, j, kk: (i, j))
    in_specs = [pl.BlockSpec((tm, tk), lambda i, j, kk: (i, kk)),
                pl.BlockSpec((tk, tn), lambda i, j, kk: (kk, j)),
                pl.BlockSpec((tm, tn), lambda i, j, kk: (i, j), pipeline_mode=pl.Buffered(1))]
    operands = [a, wb, x]
    out_specs = [tile]
    out_shape = [jax.ShapeDtypeStruct((m, n), F32)]
    if emit_next:
        rnd = _Rounder(next_stack, next_layer, grid,
                       lambda i, j, kk: (i * grid[1] + j) * grid[2] + kk)
        in_specs += [pl.BlockSpec((1, tn), lambda i, j, kk: (0, j)), rnd.in_spec]
        operands += [next_gain.reshape(1, n), next_stack]
        out_specs += [tile, pl.BlockSpec((tm, LANES), lambda i, j, kk: (i, 0)), rnd.out_spec]
        out_shape += [jax.ShapeDtypeStruct((m, n), BF16), jax.ShapeDtypeStruct((m, LANES), F32),
                      rnd.out_shape]
    return pl.pallas_call(
        functools.partial(_mlp_down_kernel, emit_next=emit_next), grid=grid,
        in_specs=in_specs, out_specs=out_specs, out_shape=out_shape,
        compiler_params=_cparams(("parallel", "arbitrary", "arbitrary")), name="mlp_down")(
            *operands)


def _layer_norm_rows(x, g, b):
    mu = jnp.mean(x, axis=-1, keepdims=True)
    xc = x - mu
    var = jnp.mean(xc * xc, axis=-1, keepdims=True)
    return xc * lax.rsqrt(var + LN_EPS) * g + b


def _gmlp_kernel(z_ref, lng_ref, lnb_ref, ws_ref, bsb_ref, o_ref, *, g, heads):
    r = z_ref.shape[0]
    z = z_ref[...]
    act = 0.5 * z * (1.0 + lax.erf(z * np.float32(math.sqrt(0.5))))
    u = act[:, :g]
    v = _layer_norm_rows(act[:, g:], lng_ref[...], lnb_ref[...]).astype(BF16)
    for c in range(r // CHUNK):
        rows = slice(c * CHUNK, (c + 1) * CHUNK)
        for h in range(heads):
            cols = slice(h * CHUNK, (h + 1) * CHUNK)
            sv = jnp.dot(ws_ref[h], v[rows, cols], preferred_element_type=F32) + bsb_ref[:, cols]
            o_ref[rows, cols] = (u[rows, cols] * sv).astype(o_ref.dtype)


def _gmlp_mixer(z, ln_g, ln_b, ws, bs, g):
    t = z.shape[0]
    heads = g // CHUNK
    r = _pick(t, 256)
    bsb = jnp.repeat(bs.T, CHUNK, axis=1)
    vec = pl.BlockSpec((1, g), lambda i: (0, 0))
    return pl.pallas_call(
        functools.partial(_gmlp_kernel, g=g, heads=heads), grid=(t // r,),
        in_specs=[pl.BlockSpec((r, 2 * g), lambda i: (i, 0)), vec, vec,
                  pl.BlockSpec((heads, CHUNK, CHUNK), lambda i: (0, 0, 0)),
                  pl.BlockSpec((CHUNK, g), lambda i: (0, 0))],
        out_specs=pl.BlockSpec((r, g), lambda i: (i, 0)),
        out_shape=jax.ShapeDtypeStruct((t, g), BF16),
        compiler_params=_cparams(("parallel",)), name="gmlp_mixer")(
            z, ln_g.reshape(1, g), ln_b.reshape(1, g), ws.astype(BF16), bsb)


def _halo_specs(r, width, col_block, seq, total):
    per_seq = seq // r
    hb = r // HALO
    last_halo_block = total // HALO - 1

    def main(b, i):
        return (b * per_seq + i, col_block)

    def prev(b, i):
        return (jnp.maximum((b * per_seq + i) * hb - 1, 0), col_block)

    def nxt(b, i):
        return (jnp.minimum((b * per_seq + i + 1) * hb, last_halo_block), col_block)

    return [pl.BlockSpec((r, width), main), pl.BlockSpec((HALO, width), prev),
            pl.BlockSpec((HALO, width), nxt)]


def _conv_kernel(z_ref, zp_ref, zn_ref, w_ref, b_ref, lng_ref, lnb_ref, wsrc_ref,
                 o_ref, wdst_ref, hbuf, cbuf, *, g):
    i = pl.program_id(1)
    _Rounder.run(wsrc_ref, wdst_ref)
    r = z_ref.shape[0]
    rb = cbuf.shape[0]
    slab_rows = rb + 2 * HALO

    def glu(zz):
        return zz[:, :g] * jax.nn.sigmoid(zz[:, g:])

    hbuf[0:HALO, :] = jnp.where(i > 0, glu(zp_ref[...]), 0.0)
    hbuf[HALO:HALO + r, :] = glu(z_ref[...])
    hbuf[HALO + r:, :] = jnp.where(i < pl.num_programs(1) - 1, glu(zn_ref[...]), 0.0)

    def group(q, carry):
        base = pl.multiple_of(q * rb, rb)
        for cb in range(g // LANES):
            cols = slice(cb * LANES, (cb + 1) * LANES)
            slab = hbuf[pl.ds(base, slab_rows), cols]
            acc = jnp.zeros((rb, LANES), F32)
            for b in range(SUBLANES):
                sh = slab if b == 0 else pltpu.roll(slab, slab_rows - b, axis=0)
                for a in range(slab_rows // SUBLANES):
                    k = SUBLANES * a + b - (HALO - CONV_PAD)
                    if 0 <= k < CONV_WIDTH:
                        acc = acc + sh[SUBLANES * a:SUBLANES * a + rb, :] * w_ref[k:k + 1, cols]
            cbuf[:, cols] = acc
        y = _layer_norm_rows(cbuf[...] + b_ref[...], lng_ref[...], lnb_ref[...])
        o_ref[pl.ds(base, rb), :] = (y * jax.nn.sigmoid(y)).astype(o_ref.dtype)
        return carry

    lax.fori_loop(0, r // rb, group, 0)


def _conv_mixer(z, w, b, ln_g, ln_b, g, batch, seq, round_stack, round_layer):
    t = z.shape[0]
    r = _pick(seq, 512)
    rb = _pick(r, CONV_ROWS)
    grid = (batch, seq // r)
    rnd = _Rounder(round_stack, round_layer, grid, lambda bb, i: bb * grid[1] + i)
    vec = pl.BlockSpec((1, g), lambda bb, i: (0, 0))
    return pl.pallas_call(
        functools.partial(_conv_kernel, g=g), grid=grid,
        in_specs=_halo_specs(r, 2 * g, 1, seq, t) + [
            pl.BlockSpec((CONV_WIDTH, g), lambda bb, i: (0, 0)), vec, vec, vec, rnd.in_spec],
        out_specs=[pl.BlockSpec((r, g), lambda bb, i: (bb * (seq // r) + i, 0)), rnd.out_spec],
        out_shape=[jax.ShapeDtypeStruct((t, g), BF16), rnd.out_shape],
        scratch_shapes=[pltpu.VMEM((r + 2 * HALO, g), F32), pltpu.VMEM((rb, g), F32)],
        compiler_params=_cparams(("parallel", "arbitrary")), name="conv_mixer")(
            z, z, z, w, b.reshape(1, g), ln_g.reshape(1, g), ln_b.reshape(1, g), round_stack)


def _pool_kernel(z_ref, zp_ref, zn_ref, pw_ref, sc_ref, o_ref, zbuf, pbuf, *, g, seq):
    i = pl.program_id(1)
    r = z_ref.shape[0]
    pc = g // len(POOL_WINDOWS)
    zbuf[0:HALO, :] = jnp.where(i > 0, zp_ref[...], 0.0)
    zbuf[HALO:HALO + r, :] = z_ref[...]
    zbuf[HALO + r:, :] = jnp.where(i < pl.num_programs(1) - 1, zn_ref[...], 0.0)
    pos = i * r + lax.broadcasted_iota(jnp.int32, (r, 1), 0)
    for gi, win in enumerate(POOL_WINDOWS):
        cols = slice(gi * pc, (gi + 1) * pc)
        half = win // 2
        tot = zbuf[HALO - half:HALO - half + r, cols]
        for d in range(1 - half, half):
            tot = tot + zbuf[HALO + d:HALO + d + r, cols]
        cnt = (jnp.minimum(pos + half, seq) - jnp.maximum(pos - half, 0)).astype(F32)
        pbuf[:, cols] = (tot / cnt - z_ref[:, cols]).astype(BF16)
    for gi in range(len(POOL_WINDOWS)):
        cols = slice(gi * pc, (gi + 1) * pc)
        y = jnp.dot(pbuf[:, cols], pw_ref[gi], preferred_element_type=F32)
        o_ref[:, cols] = (y * sc_ref[:, cols]).astype(o_ref.dtype)


def _pool_mixer(z, pool_w, scale, g, batch, seq):
    t = z.shape[0]
    r = _pick(seq, 256)
    pc = g // len(POOL_WINDOWS)
    return pl.pallas_call(
        functools.partial(_pool_kernel, g=g, seq=seq), grid=(batch, seq // r),
        in_specs=_halo_specs(r, g, 4, seq, t) + [
            pl.BlockSpec((len(POOL_WINDOWS), pc, pc), lambda bb, i: (0, 0, 0)),
            pl.BlockSpec((1, g), lambda bb, i: (0, 0))],
        out_specs=pl.BlockSpec((r, g), lambda bb, i: (bb * (seq // r) + i, 0)),
        out_shape=jax.ShapeDtypeStruct((t, g), BF16),
        scratch_shapes=[pltpu.VMEM((r + 2 * HALO, g), F32), pltpu.VMEM((r, g), BF16)],
        compiler_params=_cparams(("parallel", "arbitrary")), name="pool_mixer")(
            z, z, z, pool_w.astype(BF16), scale.reshape(1, g))


def _qk_prep_kernel(zq_ref, zk_ref, zv_ref, cos_ref, sin_ref, gq_ref, gk_ref, bd_ref,
                    q_ref, kt_ref, v_ref, *, heads):
    lane = lax.broadcasted_iota(jnp.int32, (1, HEAD_W), 1) % DIFF_HEAD_DIM
    first_half = lane < ROT_DIM // 2
    cos = cos_ref[...]
    sin = sin_ref[...]
    bd = bd_ref[...]

    def norm_rope(x, gain):
        x2 = x * x
        hi = x2.astype(BF16)
        lo = (x2 - hi.astype(F32)).astype(BF16)
        ss = (jnp.dot(hi, bd, preferred_element_type=F32)
              + jnp.dot(lo, bd, preferred_element_type=F32))
        xn = x * lax.rsqrt(ss * (1.0 / DIFF_HEAD_DIM) + RMS_EPS) * gain
        partner = jnp.where(first_half,
                            pltpu.roll(xn, HEAD_W - ROT_DIM // 2, axis=1),
                            pltpu.roll(xn, ROT_DIM // 2, axis=1))
        return xn * cos + partner * sin

    for h in range(heads):
        cols = slice(h * HEAD_W, (h + 1) * HEAD_W)
        q = norm_rope(zq_ref[:, cols], gq_ref[...]) * (DIFF_HEAD_DIM ** -0.5 * LOG2_E)
        q_ref[:, cols] = q.astype(q_ref.dtype)
        k = norm_rope(zk_ref[:, cols], gk_ref[...])
        kt_ref[0, h] = k.T.astype(kt_ref.dtype)
    v_ref[...] = zv_ref[...].astype(v_ref.dtype)


def _rope_tables(seq):
    pos = jnp.arange(seq, dtype=F32)
    inv_freq = ROPE_THETA ** (-jnp.arange(0, ROT_DIM, 2, dtype=F32) / ROT_DIM)
    ang = pos[:, None] * inv_freq[None, :]
    cos, sin = jnp.cos(ang), jnp.sin(ang)
    pad = DIFF_HEAD_DIM - ROT_DIM
    cos_c = jnp.concatenate([cos, cos, jnp.ones((seq, pad), F32)], axis=1)
    sin_c = jnp.concatenate([-sin, sin, jnp.zeros((seq, pad), F32)], axis=1)
    return jnp.tile(cos_c, (1, 2)), jnp.tile(sin_c, (1, 2))


def _qk_prep(z, qn_g, kn_g, cos_t, sin_t, g, batch, seq):
    t = z.shape[0]
    heads = g // HEAD_W
    r = _pick(seq, 256)
    per_seq = seq // r
    bd = jnp.kron(jnp.eye(2, dtype=F32), jnp.ones((DIFF_HEAD_DIM, DIFF_HEAD_DIM), F32)).astype(BF16)

    def zcol(c):
        return pl.BlockSpec((r, g), lambda i: (i, c))

    tab = pl.BlockSpec((r, HEAD_W), lambda i: (i % per_seq, 0))
    gain = pl.BlockSpec((1, HEAD_W), lambda i: (0, 0))
    row = pl.BlockSpec((r, g), lambda i: (i, 0))
    return pl.pallas_call(
        functools.partial(_qk_prep_kernel, heads=heads), grid=(t // r,),
        in_specs=[zcol(5), zcol(6), zcol(7), tab, tab, gain, gain,
                  pl.BlockSpec((HEAD_W, HEAD_W), lambda i: (0, 0))],
        out_specs=[row,
                   pl.BlockSpec((1, heads, HEAD_W, r), lambda i: (i // per_seq, 0, 0, i % per_seq)),
                   row],
        out_shape=[jax.ShapeDtypeStruct((t, g), BF16),
                   jax.ShapeDtypeStruct((batch, heads, HEAD_W, seq), BF16),
                   jax.ShapeDtypeStruct((t, g), BF16)],
        compiler_params=_cparams(("parallel",)), name="qk_prep")(
            z, z, z, cos_t, sin_t, jnp.tile(qn_g, 2).reshape(1, HEAD_W),
            jnp.tile(kn_g, 2).reshape(1, HEAD_W), bd)


def _attn_kernel(q_ref, kt_ref, v_ref, lam_ref, sg_ref, o_ref,
                 vext, s_a, s_b, mx_a, mx_b, m_run, acc, *, tq, tk, lam_init):
    seq = v_ref.shape[0]
    d = DIFF_HEAD_DIM
    nk = seq // tk
    half_nk = nk // 2
    n_iter = (seq // tq) * half_nk

    vext[:, :HEAD_W] = v_ref[...]
    vext[:, HEAD_W:] = jnp.ones((seq, HEAD_W), vext.dtype)
    m_run[...] = jnp.zeros_like(m_run)
    acc[...] = jnp.zeros_like(acc)

    lp = lam_ref[...]
    lam = (jnp.exp(jnp.sum(lp[0:1] * lp[1:2], axis=-1, keepdims=True))
           - jnp.exp(jnp.sum(lp[2:3] * lp[3:4], axis=-1, keepdims=True)) + lam_init)

    def scores(i, j, s_buf, mx_buf):
        qt = q_ref[pl.ds(pl.multiple_of(i * tq, tq), tq), :]
        keys = pl.ds(pl.multiple_of(j * tk, tk), tk)
        for c in range(2):
            s = jnp.dot(qt[:, c * d:(c + 1) * d], kt_ref[0, 0, c * d:(c + 1) * d, keys],
                        preferred_element_type=F32)
            s_buf[c] = s
            mx_buf[c] = jnp.broadcast_to(jnp.max(s, axis=-1, keepdims=True), (tq, HEAD_W))

    def accumulate(j, s_buf, mx_buf, first):
        vj = vext[pl.ds(pl.multiple_of(j * tk, tk), tk), :]
        for c in range(2):
            m_new = mx_buf[c]
            if first is None:
                m_new = jnp.maximum(m_run[c], m_new)
                alpha = jnp.exp2(m_run[c] - m_new)
                prev = acc[c] * jnp.concatenate([alpha, alpha], axis=1)
            else:
                m_old = jnp.where(first, -jnp.inf, m_run[c])
                m_new = jnp.maximum(m_old, m_new)
                alpha = jnp.exp2(m_old - m_new)
                prev = jnp.where(first, 0.0, acc[c] * jnp.concatenate([alpha, alpha], axis=1))
            m_run[c] = m_new
            p = jnp.concatenate(
                [jnp.exp2(s_buf[c, :, kb * HEAD_W:(kb + 1) * HEAD_W] - m_new).astype(BF16)
                 for kb in range(tk // HEAD_W)], axis=1)
            acc[c] = prev + jnp.dot(p, vj, preferred_element_type=F32)

    scores(0, 0, s_a, mx_a)

    def body(it, carry):
        i = it // half_nk
        j0 = 2 * (it % half_nk)
        nxt = jnp.minimum(it + 1, n_iter - 1)
        scores(i, j0 + 1, s_b, mx_b)
        accumulate(j0, s_a, mx_a, j0 == 0)
        scores(nxt // half_nk, 2 * (nxt % half_nk), s_a, mx_a)
        accumulate(j0 + 1, s_b, mx_b, None)

        @pl.when(j0 + 2 == nk)
        def _():
            a0 = acc[0]
            a1 = acc[1]
            o = a0[:, :HEAD_W] / a0[:, HEAD_W:] - lam * (a1[:, :HEAD_W] / a1[:, HEAD_W:])
            o = _rms_rows(o, sg_ref[...]) * (1.0 - lam_init)
            o_ref[pl.ds(pl.multiple_of(i * tq, tq), tq), :] = o.astype(o_ref.dtype)

        return carry

    lax.fori_loop(0, n_iter, body, 0)


def _diff_attention(q, kt, v, lam_params, subln_g, lam_init, g, batch, seq):
    t = q.shape[0]
    heads = g // HEAD_W
    tq = _pick(seq, ATTN_TQ)
    tk = _pick(seq // 2, ATTN_TK)
    return pl.pallas_call(
        functools.partial(_attn_kernel, tq=tq, tk=tk, lam_init=lam_init),
        grid=(batch, heads),
        in_specs=[pl.BlockSpec((seq, HEAD_W), lambda b, h: (b, h)),
                  pl.BlockSpec((1, 1, HEAD_W, seq), lambda b, h: (b, h, 0, 0)),
                  pl.BlockSpec((seq, HEAD_W), lambda b, h: (b, h)),
                  pl.BlockSpec((4, DIFF_HEAD_DIM), lambda b, h: (0, 0)),
                  pl.BlockSpec((1, HEAD_W), lambda b, h: (0, 0))],
        out_specs=pl.BlockSpec((seq, HEAD_W), lambda b, h: (b, h)),
        out_shape=jax.ShapeDtypeStruct((t, g), BF16),
        scratch_shapes=[pltpu.VMEM((seq, 2 * HEAD_W), BF16),
                        pltpu.VMEM((2, tq, tk), F32), pltpu.VMEM((2, tq, tk), F32),
                        pltpu.VMEM((2, tq, HEAD_W), F32), pltpu.VMEM((2, tq, HEAD_W), F32),
                        pltpu.VMEM((2, tq, HEAD_W), F32), pltpu.VMEM((2, tq, 2 * HEAD_W), F32)],
        compiler_params=_cparams(("parallel", "parallel")), name="diff_attention")(
            q, kt, v, lam_params, subln_g.reshape(1, HEAD_W))


def kernel(x, attn_norm_g, w_in, gmlp_ln_g, gmlp_ln_b, gmlp_ws, gmlp_bs, conv_w, conv_b, conv_ln_g, conv_ln_b, pool_w, pool_scale, q_norm_g, k_norm_g, lambda_q1, lambda_k1, lambda_q2, lambda_k2, subln_g, w_out, mlp_norm_g, w_up, w_down):
    batch, seq, d = x.shape
    depth = w_in.shape[0]
    g = d // 4
    t = batch * seq
    cos_t, sin_t = _rope_tables(seq)

    xr = x.reshape(t, d)
    xg, ss, wb_in = _norm0(xr, attn_norm_g[0], w_in)
    for l in range(depth):
        lam_init = 0.8 - 0.6 * math.exp(-0.3 * l)
        z, wb_up = _scaled_matmul(xg, ss, wb_in, F32, w_up, l, name="in_proj")
        y_a = _gmlp_mixer(z, gmlp_ln_g[l], gmlp_ln_b[l], gmlp_ws[l], gmlp_bs[l], g)
        y_b, wb_out = _conv_mixer(z, conv_w[l], conv_b[l], conv_ln_g[l], conv_ln_b[l], g, batch, seq,
                                  w_out, l)
        y_c = _pool_mixer(z, pool_w[l], pool_scale[l], g, batch, seq)
        q, kt, v = _qk_prep(z, q_norm_g[l], k_norm_g[l], cos_t, sin_t, g, batch, seq)
        lam_params = jnp.stack([lambda_q1[l], lambda_k1[l], lambda_q2[l], lambda_k2[l]])
        y_d = _diff_attention(q, kt, v, lam_params, subln_g[l], lam_init, g, batch, seq)
        xr, xg, ss = _out_proj((y_a, y_b, y_c, y_d), wb_out, xr, mlp_norm_g[l])
        hid, wb_down = _scaled_matmul(xg, ss, wb_up, BF16, w_down, l, relu2=True, name="mlp_up")
        if l + 1 < depth:
            xr, xg, ss, wb_in = _mlp_down(hid, wb_down, xr, attn_norm_g[l + 1], w_in, l + 1)
        else:
            (xr,) = _mlp_down(hid, wb_down, xr)
    return xr.reshape(batch, seq, d)
```

```python
import functools
import math

import jax
import jax.numpy as jnp
import numpy as np
from jax import lax
from jax.experimental import pallas as pl
from jax.experimental.pallas import tpu as pltpu

F32 = jnp.float32
BF16 = jnp.bfloat16

CHUNK = 128
CONV_WIDTH = 31
CONV_PAD = CONV_WIDTH // 2
POOL_WINDOWS = (2, 4, 8, 16)
DIFF_HEAD_DIM = 64
HEAD_W = 2 * DIFF_HEAD_DIM
ROT_DIM = DIFF_HEAD_DIM // 4
ROPE_THETA = 500000.0
LOG2_E = math.log2(math.e)
RMS_EPS = 1e-6
LN_EPS = 1e-5

LANES = 128
SUBLANES = 8

ATTN_TQ = 512
ATTN_TK = 2048
CONV_ROWS = 64
HALO = 16
VMEM_LIMIT = 58 * 1024 * 1024


def _cparams(sem):
    return pltpu.CompilerParams(dimension_semantics=sem, vmem_limit_bytes=VMEM_LIMIT)


def _pick(n, pref):
    t = min(pref, n)
    while n % t:
        t //= 2
    return t


def _rms_rows(x, g):
    ms = jnp.mean(x * x, axis=-1, keepdims=True)
    return x * lax.rsqrt(ms + RMS_EPS) * g


def _lane_partial_sumsq(x):
    x2 = x * x
    part = x2[:, 0:LANES]
    for c in range(1, x.shape[1] // LANES):
        part = part + x2[:, c * LANES:(c + 1) * LANES]
    return part


def _row_rms_scale(ss_ref, inv_d):
    return lax.rsqrt(jnp.sum(ss_ref[...], axis=-1, keepdims=True) * inv_d + RMS_EPS)


class _Rounder:
    def __init__(self, w_stack, layer, grid, step_of):
        _, k, n = w_stack.shape
        steps = math.prod(grid)
        rows = max(2 * SUBLANES, k // steps)
        row_blocks = k // rows
        col_splits = steps // row_blocks
        cols = n // col_splits
        assert rows * row_blocks == k and row_blocks * col_splits == steps, (k, n, steps)
        assert cols * col_splits == n and cols % LANES == 0, (k, n, steps)

        def block_of(*ids):
            s = step_of(*ids)
            return s // col_splits, s % col_splits

        self.in_spec = pl.BlockSpec((1, rows, cols), lambda *ids: (layer, *block_of(*ids)))
        self.out_spec = pl.BlockSpec((rows, cols), lambda *ids: block_of(*ids))
        self.out_shape = jax.ShapeDtypeStruct((k, n), BF16)

    @staticmethod
    def run(src_ref, dst_ref):
        dst_ref[...] = src_ref[0].astype(dst_ref.dtype)


def _norm0_kernel(x_ref, g_ref, wsrc_ref, xg_ref, ss_ref, wdst_ref):
    x = x_ref[...]
    xg_ref[...] = (x * g_ref[...]).astype(xg_ref.dtype)
    ss_ref[...] = _lane_partial_sumsq(x)
    _Rounder.run(wsrc_ref, wdst_ref)


def _norm0(x, g, w_stack):
    t, d = x.shape
    r = _pick(t, 256)
    grid = (t // r,)
    rnd = _Rounder(w_stack, 0, grid, lambda i: i)
    row = pl.BlockSpec((r, d), lambda i: (i, 0))
    return pl.pallas_call(
        _norm0_kernel, grid=grid,
        in_specs=[row, pl.BlockSpec((1, d), lambda i: (0, 0)), rnd.in_spec],
        out_specs=[row, pl.BlockSpec((r, LANES), lambda i: (i, 0)), rnd.out_spec],
        out_shape=[jax.ShapeDtypeStruct((t, d), BF16), jax.ShapeDtypeStruct((t, LANES), F32),
                   rnd.out_shape],
        compiler_params=_cparams(("parallel",)), name="norm0")(x, g.reshape(1, d), w_stack)


def _scaled_mm_kernel(a_ref, w_ref, ss_ref, wsrc_ref, o_ref, wdst_ref, *, relu2, inv_d):
    acc = jnp.dot(a_ref[...], w_ref[...], preferred_element_type=F32) * _row_rms_scale(ss_ref, inv_d)
    if relu2:
        acc = jnp.square(jnp.maximum(acc, 0.0))
    o_ref[...] = acc.astype(o_ref.dtype)
    _Rounder.run(wsrc_ref, wdst_ref)


def _scaled_matmul(xg, ss, wb, out_dtype, round_stack, round_layer, *, relu2=False, name):
    m, k = xg.shape
    n = wb.shape[1]
    tm, tn = _pick(m, 1024), _pick(n, 1024)
    grid = (m // tm, n // tn)
    rnd = _Rounder(round_stack, round_layer, grid, lambda i, j: i * grid[1] + j)
    return pl.pallas_call(
        functools.partial(_scaled_mm_kernel, relu2=relu2, inv_d=1.0 / k), grid=grid,
        in_specs=[pl.BlockSpec((tm, k), lambda i, j: (i, 0)),
                  pl.BlockSpec((k, tn), lambda i, j: (0, j)),
                  pl.BlockSpec((tm, LANES), lambda i, j: (i, 0)), rnd.in_spec],
        out_specs=[pl.BlockSpec((tm, tn), lambda i, j: (i, j)), rnd.out_spec],
        out_shape=[jax.ShapeDtypeStruct((m, n), out_dtype), rnd.out_shape],
        compiler_params=_cparams(("parallel", "parallel")), name=name)(xg, wb, ss, round_stack)


def _emit_norm_inputs(xn, g_ref, xg_ref, ss_ref, first):
    xg_ref[...] = (xn * g_ref[...]).astype(xg_ref.dtype)
    part = _lane_partial_sumsq(xn)

    @pl.when(first)
    def _():
        ss_ref[...] = part

    @pl.when(jnp.logical_not(first))
    def _():
        ss_ref[...] += part


def _out_proj_kernel(a0_ref, a1_ref, a2_ref, a3_ref, w_ref, x_ref, g_ref, o_ref, xg_ref, ss_ref,
                     *, g):
    acc = x_ref[...]
    for idx, a_ref in enumerate((a0_ref, a1_ref, a2_ref, a3_ref)):
        acc += jnp.dot(a_ref[...], w_ref[idx * g:(idx + 1) * g, :], preferred_element_type=F32)
    o_ref[...] = acc
    _emit_norm_inputs(acc, g_ref, xg_ref, ss_ref, pl.program_id(1) == 0)


def _out_proj(ys, wb, x, gain):
    m, g = ys[0].shape
    n = wb.shape[1]
    tm, tn = _pick(m, 1024), _pick(n, 512)
    a_spec = pl.BlockSpec((tm, g), lambda i, j: (i, 0))
    tile = pl.BlockSpec((tm, tn), lambda i, j: (i, j))
    return pl.pallas_call(
        functools.partial(_out_proj_kernel, g=g), grid=(m // tm, n // tn),
        in_specs=[a_spec, a_spec, a_spec, a_spec, pl.BlockSpec((4 * g, tn), lambda i, j: (0, j)),
                  tile, pl.BlockSpec((1, tn), lambda i, j: (0, j))],
        out_specs=[tile, tile, pl.BlockSpec((tm, LANES), lambda i, j: (i, 0))],
        out_shape=[jax.ShapeDtypeStruct((m, n), F32), jax.ShapeDtypeStruct((m, n), BF16),
                   jax.ShapeDtypeStruct((m, LANES), F32)],
        compiler_params=_cparams(("parallel", "arbitrary")), name="out_proj")(
            *ys, wb, x, gain.reshape(1, n))


def _mlp_down_kernel(a_ref, w_ref, x_ref, *rest, emit_next):
    kk = pl.program_id(2)
    o_ref = rest[2] if emit_next else rest[0]

    @pl.when(kk == 0)
    def _():
        o_ref[...] = x_ref[...]

    o_ref[...] += jnp.dot(a_ref[...], w_ref[...], preferred_element_type=F32)

    if emit_next:
        g_ref, wsrc_ref, _, xg_ref, ss_ref, wdst_ref = rest
        first_col = pl.program_id(1) == 0

        @pl.when(kk == pl.num_programs(2) - 1)
        def _():
            _emit_norm_inputs(o_ref[...], g_ref, xg_ref, ss_ref, first_col)

        _Rounder.run(wsrc_ref, wdst_ref)


def _mlp_down(a, wb, x, next_gain=None, next_stack=None, next_layer=None):
    m, k = a.shape
    n = wb.shape[1]
    tm, tn, tk = _pick(m, 1024), _pick(n, 1024), _pick(k, 2048)
    grid = (m // tm, n // tn, k // tk)
    emit_next = next_gain is not None
    tile = pl.BlockSpec((tm, tn), lambda i, j, kk: (i, j))
    in_specs = [pl.BlockSpec((tm, tk), lambda i, j, kk: (i, kk)),
                pl.BlockSpec((tk, tn), lambda i, j, kk: (kk, j)),
                tile]
    operands = [a, wb, x]
    out_specs = [tile]
    out_shape = [jax.ShapeDtypeStruct((m, n), F32)]
    if emit_next:
        rnd = _Rounder(next_stack, next_layer, grid,
                       lambda i, j, kk: (i * grid[1] + j) * grid[2] + kk)
        in_specs += [pl.BlockSpec((1, tn), lambda i, j, kk: (0, j)), rnd.in_spec]
        operands += [next_gain.reshape(1, n), next_stack]
        out_specs += [tile, pl.BlockSpec((tm, LANES), lambda i, j, kk: (i, 0)), rnd.out_spec]
        out_shape += [jax.ShapeDtypeStruct((m, n), BF16), jax.ShapeDtypeStruct((m, LANES), F32),
                      rnd.out_shape]
    return pl.pallas_call(
        functools.partial(_mlp_down_kernel, emit_next=emit_next), grid=grid,
        in_specs=in_specs, out_specs=out_specs, out_shape=out_shape,
        compiler_params=_cparams(("parallel", "arbitrary", "arbitrary")), name="mlp_down")(
            *operands)


def _layer_norm_rows(x, g, b):
    mu = jnp.mean(x, axis=-1, keepdims=True)
    xc = x - mu
    var = jnp.mean(xc * xc, axis=-1, keepdims=True)
    return xc * lax.rsqrt(var + LN_EPS) * g + b


def _gmlp_kernel(z_ref, lng_ref, lnb_ref, ws_ref, bsb_ref, o_ref, *, g, heads):
    r = z_ref.shape[0]
    z = z_ref[...]
    act = 0.5 * z * (1.0 + lax.erf(z * np.float32(math.sqrt(0.5))))
    u = act[:, :g]
    v = _layer_norm_rows(act[:, g:], lng_ref[...], lnb_ref[...]).astype(BF16)
    for c in range(r // CHUNK):
        rows = slice(c * CHUNK, (c + 1) * CHUNK)
        for h in range(heads):
            cols = slice(h * CHUNK, (h + 1) * CHUNK)
            sv = jnp.dot(ws_ref[h], v[rows, cols], preferred_element_type=F32) + bsb_ref[:, cols]
            o_ref[rows, cols] = (u[rows, cols] * sv).astype(o_ref.dtype)


def _gmlp_mixer(z, ln_g, ln_b, ws, bs, g):
    t = z.shape[0]
    heads = g // CHUNK
    r = _pick(t, 512)
    bsb = jnp.repeat(bs.T, CHUNK, axis=1)
    vec = pl.BlockSpec((1, g), lambda i: (0, 0))
    return pl.pallas_call(
        functools.partial(_gmlp_kernel, g=g, heads=heads), grid=(t // r,),
        in_specs=[pl.BlockSpec((r, 2 * g), lambda i: (i, 0)), vec, vec,
                  pl.BlockSpec((heads, CHUNK, CHUNK), lambda i: (0, 0, 0)),
                  pl.BlockSpec((CHUNK, g), lambda i: (0, 0))],
        out_specs=pl.BlockSpec((r, g), lambda i: (i, 0)),
        out_shape=jax.ShapeDtypeStruct((t, g), BF16),
        compiler_params=_cparams(("parallel",)), name="gmlp_mixer")(
            z, ln_g.reshape(1, g), ln_b.reshape(1, g), ws.astype(BF16), bsb)


def _halo_specs(r, width, col_block, seq, total):
    per_seq = seq // r
    hb = r // HALO
    last_halo_block = total // HALO - 1

    def main(b, i):
        return (b * per_seq + i, col_block)

    def prev(b, i):
        return (jnp.maximum((b * per_seq + i) * hb - 1, 0), col_block)

    def nxt(b, i):
        return (jnp.minimum((b * per_seq + i + 1) * hb, last_halo_block), col_block)

    return [pl.BlockSpec((r, width), main), pl.BlockSpec((HALO, width), prev),
            pl.BlockSpec((HALO, width), nxt)]


def _conv_kernel(z_ref, zp_ref, zn_ref, w_ref, b_ref, lng_ref, lnb_ref, wsrc_ref,
                 o_ref, wdst_ref, hbuf, cbuf, *, g):
    i = pl.program_id(1)
    _Rounder.run(wsrc_ref, wdst_ref)
    r = z_ref.shape[0]
    rb = cbuf.shape[0]
    slab_rows = rb + 2 * HALO

    def glu(zz):
        return zz[:, :g] * jax.nn.sigmoid(zz[:, g:])

    hbuf[0:HALO, :] = jnp.where(i > 0, glu(zp_ref[...]), 0.0)
    hbuf[HALO:HALO + r, :] = glu(z_ref[...])
    hbuf[HALO + r:, :] = jnp.where(i < pl.num_programs(1) - 1, glu(zn_ref[...]), 0.0)

    def group(q, carry):
        base = pl.multiple_of(q * rb, rb)
        for cb in range(g // LANES):
            cols = slice(cb * LANES, (cb + 1) * LANES)
            slab = hbuf[pl.ds(base, slab_rows), cols]
            acc = jnp.zeros((rb, LANES), F32)
            for b in range(SUBLANES):
                sh = slab if b == 0 else pltpu.roll(slab, slab_rows - b, axis=0)
                for a in range(slab_rows // SUBLANES):
                    k = SUBLANES * a + b - (HALO - CONV_PAD)
                    if 0 <= k < CONV_WIDTH:
                        acc = acc + sh[SUBLANES * a:SUBLANES * a + rb, :] * w_ref[k:k + 1, cols]
            cbuf[:, cols] = acc
        y = _layer_norm_rows(cbuf[...] + b_ref[...], lng_ref[...], lnb_ref[...])
        o_ref[pl.ds(base, rb), :] = (y * jax.nn.sigmoid(y)).astype(o_ref.dtype)
        return carry

    lax.fori_loop(0, r // rb, group, 0)


def _conv_mixer(z, w, b, ln_g, ln_b, g, batch, seq, round_stack, round_layer):
    t = z.shape[0]
    r = _pick(seq, 512)
    rb = _pick(r, CONV_ROWS)
    grid = (batch, seq // r)
    rnd = _Rounder(round_stack, round_layer, grid, lambda bb, i: bb * grid[1] + i)
    vec = pl.BlockSpec((1, g), lambda bb, i: (0, 0))
    return pl.pallas_call(
        functools.partial(_conv_kernel, g=g), grid=grid,
        in_specs=_halo_specs(r, 2 * g, 1, seq, t) + [
            pl.BlockSpec((CONV_WIDTH, g), lambda bb, i: (0, 0)), vec, vec, vec, rnd.in_spec],
        out_specs=[pl.BlockSpec((r, g), lambda bb, i: (bb * (seq // r) + i, 0)), rnd.out_spec],
        out_shape=[jax.ShapeDtypeStruct((t, g), BF16), rnd.out_shape],
        scratch_shapes=[pltpu.VMEM((r + 2 * HALO, g), F32), pltpu.VMEM((rb, g), F32)],
        compiler_params=_cparams(("parallel", "arbitrary")), name="conv_mixer")(
            z, z, z, w, b.reshape(1, g), ln_g.reshape(1, g), ln_b.reshape(1, g), round_stack)


def _pool_kernel(z_ref, zp_ref, zn_ref, pw_ref, sc_ref, o_ref, zbuf, pbuf, *, g, seq):
    i = pl.program_id(1)
    r = z_ref.shape[0]
    pc = g // len(POOL_WINDOWS)
    zbuf[0:HALO, :] = jnp.where(i > 0, zp_ref[...], 0.0)
    zbuf[HALO:HALO + r, :] = z_ref[...]
    zbuf[HALO + r:, :] = jnp.where(i < pl.num_programs(1) - 1, zn_ref[...], 0.0)
    pos = i * r + lax.broadcasted_iota(jnp.int32, (r, 1), 0)
    for gi, win in enumerate(POOL_WINDOWS):
        cols = slice(gi * pc, (gi + 1) * pc)
        half = win // 2
        tot = zbuf[HALO - half:HALO - half + r, cols]
        for d in range(1 - half, half):
            tot = tot + zbuf[HALO + d:HALO + d + r, cols]
        cnt = (jnp.minimum(pos + half, seq) - jnp.maximum(pos - half, 0)).astype(F32)
        pbuf[:, cols] = (tot / cnt - z_ref[:, cols]).astype(BF16)
    for gi in range(len(POOL_WINDOWS)):
        cols = slice(gi * pc, (gi + 1) * pc)
        y = jnp.dot(pbuf[:, cols], pw_ref[gi], preferred_element_type=F32)
        o_ref[:, cols] = (y * sc_ref[:, cols]).astype(o_ref.dtype)


def _pool_mixer(z, pool_w, scale, g, batch, seq):
    t = z.shape[0]
    r = _pick(seq, 512)
    pc = g // len(POOL_WINDOWS)
    return pl.pallas_call(
        functools.partial(_pool_kernel, g=g, seq=seq), grid=(batch, seq // r),
        in_specs=_halo_specs(r, g, 4, seq, t) + [
            pl.BlockSpec((len(POOL_WINDOWS), pc, pc), lambda bb, i: (0, 0, 0)),
            pl.BlockSpec((1, g), lambda bb, i: (0, 0))],
        out_specs=pl.BlockSpec((r, g), lambda bb, i: (bb * (seq // r) + i, 0)),
        out_shape=jax.ShapeDtypeStruct((t, g), BF16),
        scratch_shapes=[pltpu.VMEM((r + 2 * HALO, g), F32), pltpu.VMEM((r, g), BF16)],
        compiler_params=_cparams(("parallel", "arbitrary")), name="pool_mixer")(
            z, z, z, pool_w.astype(BF16), scale.reshape(1, g))


def _qk_prep_kernel(zq_ref, zk_ref, zv_ref, cos_ref, sin_ref, gq_ref, gk_ref, bd_ref,
                    q_ref, kt_ref, v_ref, *, heads):
    lane = lax.broadcasted_iota(jnp.int32, (1, HEAD_W), 1) % DIFF_HEAD_DIM
    first_half = lane < ROT_DIM // 2
    cos = cos_ref[...]
    sin = sin_ref[...]
    bd = bd_ref[...]

    def norm_rope(x, gain):
        x2 = x * x
        hi = x2.astype(BF16)
        lo = (x2 - hi.astype(F32)).astype(BF16)
        ss = (jnp.dot(hi, bd, preferred_element_type=F32)
              + jnp.dot(lo, bd, preferred_element_type=F32))
        xn = x * lax.rsqrt(ss * (1.0 / DIFF_HEAD_DIM) + RMS_EPS) * gain
        partner = jnp.where(first_half,
                            pltpu.roll(xn, HEAD_W - ROT_DIM // 2, axis=1),
                            pltpu.roll(xn, ROT_DIM // 2, axis=1))
        return xn * cos + partner * sin

    for h in range(heads):
        cols = slice(h * HEAD_W, (h + 1) * HEAD_W)
        q = norm_rope(zq_ref[:, cols], gq_ref[...]) * (DIFF_HEAD_DIM ** -0.5 * LOG2_E)
        q_ref[:, cols] = q.astype(q_ref.dtype)
        k = norm_rope(zk_ref[:, cols], gk_ref[...])
        kt_ref[0, h] = k.T.astype(kt_ref.dtype)
    v_ref[...] = zv_ref[...].astype(v_ref.dtype)


def _rope_tables(seq):
    pos = jnp.arange(seq, dtype=F32)
    inv_freq = ROPE_THETA ** (-jnp.arange(0, ROT_DIM, 2, dtype=F32) / ROT_DIM)
    ang = pos[:, None] * inv_freq[None, :]
    cos, sin = jnp.cos(ang), jnp.sin(ang)
    pad = DIFF_HEAD_DIM - ROT_DIM
    cos_c = jnp.concatenate([cos, cos, jnp.ones((seq, pad), F32)], axis=1)
    sin_c = jnp.concatenate([-sin, sin, jnp.zeros((seq, pad), F32)], axis=1)
    return jnp.tile(cos_c, (1, 2)), jnp.tile(sin_c, (1, 2))


def _qk_prep(z, qn_g, kn_g, cos_t, sin_t, g, batch, seq):
    t = z.shape[0]
    heads = g // HEAD_W
    r = _pick(seq, 512)
    per_seq = seq // r
    bd = jnp.kron(jnp.eye(2, dtype=F32), jnp.ones((DIFF_HEAD_DIM, DIFF_HEAD_DIM), F32)).astype(BF16)

    def zcol(c):
        return pl.BlockSpec((r, g), lambda i: (i, c))

    tab = pl.BlockSpec((r, HEAD_W), lambda i: (i % per_seq, 0))
    gain = pl.BlockSpec((1, HEAD_W), lambda i: (0, 0))
    row = pl.BlockSpec((r, g), lambda i: (i, 0))
    return pl.pallas_call(
        functools.partial(_qk_prep_kernel, heads=heads), grid=(t // r,),
        in_specs=[zcol(5), zcol(6), zcol(7), tab, tab, gain, gain,
                  pl.BlockSpec((HEAD_W, HEAD_W), lambda i: (0, 0))],
        out_specs=[row,
                   pl.BlockSpec((1, heads, HEAD_W, r), lambda i: (i // per_seq, 0, 0, i % per_seq)),
                   row],
        out_shape=[jax.ShapeDtypeStruct((t, g), BF16),
                   jax.ShapeDtypeStruct((batch, heads, HEAD_W, seq), BF16),
                   jax.ShapeDtypeStruct((t, g), BF16)],
        compiler_params=_cparams(("parallel",)), name="qk_prep")(
            z, z, z, cos_t, sin_t, jnp.tile(qn_g, 2).reshape(1, HEAD_W),
            jnp.tile(kn_g, 2).reshape(1, HEAD_W), bd)


def _attn_kernel(q_ref, kt_ref, v_ref, lam_ref, sg_ref, o_ref,
                 vext, s_a, s_b, mx_a, mx_b, m_run, acc, *, tq, tk, lam_init):
    seq = v_ref.shape[0]
    d = DIFF_HEAD_DIM
    nk = seq // tk
    n_tiles = seq // tq

    vext[:, :HEAD_W] = v_ref[...]
    vext[:, HEAD_W:] = jnp.ones((seq, HEAD_W), vext.dtype)

    lp = lam_ref[...]
    lam = (jnp.exp(jnp.sum(lp[0:1] * lp[1:2], axis=-1, keepdims=True))
           - jnp.exp(jnp.sum(lp[2:3] * lp[3:4], axis=-1, keepdims=True)) + lam_init)

    def scores(i, j, s_buf, mx_buf):
        qt = q_ref[pl.ds(pl.multiple_of(i * tq, tq), tq), :]
        for c in range(2):
            s = jnp.dot(qt[:, c * d:(c + 1) * d],
                        kt_ref[0, 0, c * d:(c + 1) * d, j * tk:(j + 1) * tk],
                        preferred_element_type=F32)
            s_buf[c] = s
            mx_buf[c] = jnp.broadcast_to(jnp.max(s, axis=-1, keepdims=True), (tq, HEAD_W))

    def accumulate(j, s_buf, mx_buf):
        vj = vext[j * tk:(j + 1) * tk, :]
        for c in range(2):
            m_new = mx_buf[c]
            if j > 0:
                m_new = jnp.maximum(m_run[c], m_new)
                alpha = jnp.exp2(m_run[c] - m_new)
            m_run[c] = m_new
            p = jnp.concatenate(
                [jnp.exp2(s_buf[c, :, kb * HEAD_W:(kb + 1) * HEAD_W] - m_new).astype(BF16)
                 for kb in range(tk // HEAD_W)], axis=1)
            pv = jnp.dot(p, vj, preferred_element_type=F32)
            acc[c] = pv if j == 0 else acc[c] * jnp.concatenate([alpha, alpha], axis=1) + pv

    scores(0, 0, s_a, mx_a)

    def tile(i, carry):
        bufs = ((s_a, mx_a), (s_b, mx_b))
        nxt = jnp.minimum(i + 1, n_tiles - 1)
        for j in range(nk):
            if j + 1 < nk:
                scores(i, j + 1, *bufs[(j + 1) % 2])
            else:
                scores(nxt, 0, *bufs[0])
            accumulate(j, *bufs[j % 2])
        a0 = acc[0]
        a1 = acc[1]
        o = a0[:, :HEAD_W] / a0[:, HEAD_W:] - lam * (a1[:, :HEAD_W] / a1[:, HEAD_W:])
        o = _rms_rows(o, sg_ref[...]) * (1.0 - lam_init)
        o_ref[pl.ds(pl.multiple_of(i * tq, tq), tq), :] = o.astype(o_ref.dtype)
        return carry

    lax.fori_loop(0, n_tiles, tile, 0)


def _diff_attention(q, kt, v, lam_params, subln_g, lam_init, g, batch, seq):
    t = q.shape[0]
    heads = g // HEAD_W
    tq = _pick(seq, ATTN_TQ)
    tk = _pick(seq // 2, ATTN_TK)
    return pl.pallas_call(
        functools.partial(_attn_kernel, tq=tq, tk=tk, lam_init=lam_init),
        grid=(batch, heads),
        in_specs=[pl.BlockSpec((seq, HEAD_W), lambda b, h: (b, h)),
                  pl.BlockSpec((1, 1, HEAD_W, seq), lambda b, h: (b, h, 0, 0)),
                  pl.BlockSpec((seq, HEAD_W), lambda b, h: (b, h)),
                  pl.BlockSpec((4, DIFF_HEAD_DIM), lambda b, h: (0, 0)),
                  pl.BlockSpec((1, HEAD_W), lambda b, h: (0, 0))],
        out_specs=pl.BlockSpec((seq, HEAD_W), lambda b, h: (b, h)),
        out_shape=jax.ShapeDtypeStruct((t, g), BF16),
        scratch_shapes=[pltpu.VMEM((seq, 2 * HEAD_W), BF16),
                        pltpu.VMEM((2, tq, tk), F32), pltpu.VMEM((2, tq, tk), F32),
                        pltpu.VMEM((2, tq, HEAD_W), F32), pltpu.VMEM((2, tq, HEAD_W), F32),
                        pltpu.VMEM((2, tq, HEAD_W), F32), pltpu.VMEM((2, tq, 2 * HEAD_W), F32)],
        compiler_params=_cparams(("parallel", "parallel")), name="diff_attention")(
            q, kt, v, lam_params, subln_g.reshape(1, HEAD_W))


def kernel(x, attn_norm_g, w_in, gmlp_ln_g, gmlp_ln_b, gmlp_ws, gmlp_bs, conv_w, conv_b, conv_ln_g, conv_ln_b, pool_w, pool_scale, q_norm_g, k_norm_g, lambda_q1, lambda_k1, lambda_q2, lambda_k2, subln_g, w_out, mlp_norm_g, w_up, w_down):
    batch, seq, d = x.shape
    depth = w_in.shape[0]
    g = d // 4
    t = batch * seq
    cos_t, sin_t = _rope_tables(seq)

    xr = x.reshape(t, d)
    xg, ss, wb_in = _norm0(xr, attn_norm_g[0], w_in)
    for l in range(depth):
        lam_init = 0.8 - 0.6 * math.exp(-0.3 * l)
        z, wb_up = _scaled_matmul(xg, ss, wb_in, F32, w_up, l, name="in_proj")
        y_a = _gmlp_mixer(z, gmlp_ln_g[l], gmlp_ln_b[l], gmlp_ws[l], gmlp_bs[l], g)
        y_b, wb_out = _conv_mixer(z, conv_w[l], conv_b[l], conv_ln_g[l], conv_ln_b[l], g, batch, seq,
                                  w_out, l)
        y_c = _pool_mixer(z, pool_w[l], pool_scale[l], g, batch, seq)
        q, kt, v = _qk_prep(z, q_norm_g[l], k_norm_g[l], cos_t, sin_t, g, batch, seq)
        lam_params = jnp.stack([lambda_q1[l], lambda_k1[l], lambda_q2[l], lambda_k2[l]])
        y_d = _diff_attention(q, kt, v, lam_params, subln_g[l], lam_init, g, batch, seq)
        xr, xg, ss = _out_proj((y_a, y_b, y_c, y_d), wb_out, xr, mlp_norm_g[l])
        hid, wb_down = _scaled_matmul(xg, ss, wb_up, BF16, w_down, l, relu2=True, name="mlp_up")
        if l + 1 < depth:
            xr, xg, ss, wb_in = _mlp_down(hid, wb_down, xr, attn_norm_g[l + 1], w_in, l + 1)
        else:
            (xr,) = _mlp_down(hid, wb_down, xr)
    return xr.reshape(batch, seq, d)
```

```python
import functools
import math

import jax
import jax.numpy as jnp
import numpy as np
from jax import lax
from jax.experimental import pallas as pl
from jax.experimental.pallas import tpu as pltpu

F32 = jnp.float32
BF16 = jnp.bfloat16

CHUNK = 128
CONV_WIDTH = 31
CONV_PAD = CONV_WIDTH // 2
POOL_WINDOWS = (2, 4, 8, 16)
DIFF_HEAD_DIM = 64
HEAD_W = 2 * DIFF_HEAD_DIM
ROT_DIM = DIFF_HEAD_DIM // 4
ROPE_THETA = 500000.0
LOG2_E = math.log2(math.e)
RMS_EPS = 1e-6
LN_EPS = 1e-5

LANES = 128
SUBLANES = 8

ATTN_TQ = 512
ATTN_TK = 2048
CONV_ROWS = 64
HALO = 16
VMEM_LIMIT = 58 * 1024 * 1024


def _cparams(sem):
    return pltpu.CompilerParams(dimension_semantics=sem, vmem_limit_bytes=VMEM_LIMIT)


def _pick(n, pref):
    t = min(pref, n)
    while n % t:
        t //= 2
    return t


def _rms_rows(x, g):
    ms = jnp.mean(x * x, axis=-1, keepdims=True)
    return x * lax.rsqrt(ms + RMS_EPS) * g


def _lane_partial_sumsq(x):
    x2 = x * x
    part = x2[:, 0:LANES]
    for c in range(1, x.shape[1] // LANES):
        part = part + x2[:, c * LANES:(c + 1) * LANES]
    return part


def _row_rms_scale(ss_ref, inv_d):
    return lax.rsqrt(jnp.sum(ss_ref[...], axis=-1, keepdims=True) * inv_d + RMS_EPS)


class _Rounder:
    def __init__(self, w_stack, layer, grid, step_of):
        _, k, n = w_stack.shape
        steps = math.prod(grid)
        rows = max(2 * SUBLANES, k // steps)
        row_blocks = k // rows
        col_splits = steps // row_blocks
        cols = n // col_splits
        assert rows * row_blocks == k and row_blocks * col_splits == steps, (k, n, steps)
        assert cols * col_splits == n and cols % LANES == 0, (k, n, steps)

        def block_of(*ids):
            s = step_of(*ids)
            return s // col_splits, s % col_splits

        self.in_spec = pl.BlockSpec((1, rows, cols), lambda *ids: (layer, *block_of(*ids)))
        self.out_spec = pl.BlockSpec((rows, cols), lambda *ids: block_of(*ids))
        self.out_shape = jax.ShapeDtypeStruct((k, n), BF16)

    @staticmethod
    def run(src_ref, dst_ref):
        dst_ref[...] = src_ref[0].astype(dst_ref.dtype)


def _norm0_kernel(x_ref, g_ref, wsrc_ref, xg_ref, ss_ref, wdst_ref):
    x = x_ref[...]
    xg_ref[...] = (x * g_ref[...]).astype(xg_ref.dtype)
    ss_ref[...] = _lane_partial_sumsq(x)
    _Rounder.run(wsrc_ref, wdst_ref)


def _norm0(x, g, w_stack):
    t, d = x.shape
    r = _pick(t, 256)
    grid = (t // r,)
    rnd = _Rounder(w_stack, 0, grid, lambda i: i)
    row = pl.BlockSpec((r, d), lambda i: (i, 0))
    return pl.pallas_call(
        _norm0_kernel, grid=grid,
        in_specs=[row, pl.BlockSpec((1, d), lambda i: (0, 0)), rnd.in_spec],
        out_specs=[row, pl.BlockSpec((r, LANES), lambda i: (i, 0)), rnd.out_spec],
        out_shape=[jax.ShapeDtypeStruct((t, d), BF16), jax.ShapeDtypeStruct((t, LANES), F32),
                   rnd.out_shape],
        compiler_params=_cparams(("parallel",)), name="norm0")(x, g.reshape(1, d), w_stack)


def _scaled_mm_kernel(a_ref, w_ref, ss_ref, wsrc_ref, o_ref, wdst_ref, *, relu2, inv_d):
    acc = jnp.dot(a_ref[...], w_ref[...], preferred_element_type=F32) * _row_rms_scale(ss_ref, inv_d)
    if relu2:
        acc = jnp.square(jnp.maximum(acc, 0.0))
    o_ref[...] = acc.astype(o_ref.dtype)
    _Rounder.run(wsrc_ref, wdst_ref)


def _scaled_matmul(xg, ss, wb, out_dtype, round_stack, round_layer, *, relu2=False, name):
    m, k = xg.shape
    n = wb.shape[1]
    tm, tn = _pick(m, 1024), _pick(n, 1024)
    grid = (m // tm, n // tn)
    rnd = _Rounder(round_stack, round_layer, grid, lambda i, j: i * grid[1] + j)
    return pl.pallas_call(
        functools.partial(_scaled_mm_kernel, relu2=relu2, inv_d=1.0 / k), grid=grid,
        in_specs=[pl.BlockSpec((tm, k), lambda i, j: (i, 0)),
                  pl.BlockSpec((k, tn), lambda i, j: (0, j)),
                  pl.BlockSpec((tm, LANES), lambda i, j: (i, 0)), rnd.in_spec],
        out_specs=[pl.BlockSpec((tm, tn), lambda i, j: (i, j)), rnd.out_spec],
        out_shape=[jax.ShapeDtypeStruct((m, n), out_dtype), rnd.out_shape],
        compiler_params=_cparams(("parallel", "parallel")), name=name)(xg, wb, ss, round_stack)


def _emit_norm_inputs(xn, g_ref, xg_ref, ss_ref, first):
    xg_ref[...] = (xn * g_ref[...]).astype(xg_ref.dtype)
    part = _lane_partial_sumsq(xn)

    @pl.when(first)
    def _():
        ss_ref[...] = part

    @pl.when(jnp.logical_not(first))
    def _():
        ss_ref[...] += part


def _out_proj_kernel(a0_ref, a1_ref, a2_ref, a3_ref, w_ref, x_ref, g_ref, o_ref, xg_ref, ss_ref,
                     *, g):
    acc = x_ref[...]
    for idx, a_ref in enumerate((a0_ref, a1_ref, a2_ref, a3_ref)):
        acc += jnp.dot(a_ref[...], w_ref[idx * g:(idx + 1) * g, :], preferred_element_type=F32)
    o_ref[...] = acc
    _emit_norm_inputs(acc, g_ref, xg_ref, ss_ref, pl.program_id(1) == 0)


def _out_proj(ys, wb, x, gain):
    m, g = ys[0].shape
    n = wb.shape[1]
    tm, tn = _pick(m, 1024), _pick(n, 512)
    a_spec = pl.BlockSpec((tm, g), lambda i, j: (i, 0))
    tile = pl.BlockSpec((tm, tn), lambda i, j: (i, j))
    return pl.pallas_call(
        functools.partial(_out_proj_kernel, g=g), grid=(m // tm, n // tn),
        in_specs=[a_spec, a_spec, a_spec, a_spec, pl.BlockSpec((4 * g, tn), lambda i, j: (0, j)),
                  tile, pl.BlockSpec((1, tn), lambda i, j: (0, j))],
        out_specs=[tile, tile, pl.BlockSpec((tm, LANES), lambda i, j: (i, 0))],
        out_shape=[jax.ShapeDtypeStruct((m, n), F32), jax.ShapeDtypeStruct((m, n), BF16),
                   jax.ShapeDtypeStruct((m, LANES), F32)],
        compiler_params=_cparams(("parallel", "arbitrary")), name="out_proj")(
            *ys, wb, x, gain.reshape(1, n))


def _mlp_down_kernel(a_ref, w_ref, x_ref, *rest, emit_next):
    kk = pl.program_id(2)
    o_ref = rest[2] if emit_next else rest[0]

    @pl.when(kk == 0)
    def _():
        o_ref[...] = x_ref[...]

    o_ref[...] += jnp.dot(a_ref[...], w_ref[...], preferred_element_type=F32)

    if emit_next:
        g_ref, wsrc_ref, _, xg_ref, ss_ref, wdst_ref = rest
        first_col = pl.program_id(1) == 0

        @pl.when(kk == pl.num_programs(2) - 1)
        def _():
            _emit_norm_inputs(o_ref[...], g_ref, xg_ref, ss_ref, first_col)

        _Rounder.run(wsrc_ref, wdst_ref)


def _mlp_down(a, wb, x, next_gain=None, next_stack=None, next_layer=None):
    m, k = a.shape
    n = wb.shape[1]
    emit_next = next_gain is not None
    tm, tn, tk = _pick(m, 1024), _pick(n, 1024), _pick(k, 2048 if emit_next else 4096)
    grid = (m // tm, n // tn, k // tk)
    tile = pl.BlockSpec((tm, tn), lambda i, j, kk: (i, j))
    in_specs = [pl.BlockSpec((tm, tk), lambda i, j, kk: (i, kk)),
                pl.BlockSpec((tk, tn), lambda i, j, kk: (kk, j)),
                tile]
    operands = [a, wb, x]
    out_specs = [tile]
    out_shape = [jax.ShapeDtypeStruct((m, n), F32)]
    if emit_next:
        rnd = _Rounder(next_stack, next_layer, grid,
                       lambda i, j, kk: (i * grid[1] + j) * grid[2] + kk)
        in_specs += [pl.BlockSpec((1, tn), lambda i, j, kk: (0, j)), rnd.in_spec]
        operands += [next_gain.reshape(1, n), next_stack]
        out_specs += [tile, pl.BlockSpec((tm, LANES), lambda i, j, kk: (i, 0)), rnd.out_spec]
        out_shape += [jax.ShapeDtypeStruct((m, n), BF16), jax.ShapeDtypeStruct((m, LANES), F32),
                      rnd.out_shape]
    return pl.pallas_call(
        functools.partial(_mlp_down_kernel, emit_next=emit_next), grid=grid,
        in_specs=in_specs, out_specs=out_specs, out_shape=out_shape,
        compiler_params=_cparams(("parallel", "arbitrary", "arbitrary")), name="mlp_down")(
            *operands)


def _layer_norm_rows(x, g, b):
    mu = jnp.mean(x, axis=-1, keepdims=True)
    xc = x - mu
    var = jnp.mean(xc * xc, axis=-1, keepdims=True)
    return xc * lax.rsqrt(var + LN_EPS) * g + b


def _gmlp_kernel(z_ref, lng_ref, lnb_ref, ws_ref, bsb_ref, o_ref, *, g, heads):
    r = z_ref.shape[0]
    z = z_ref[...]
    act = 0.5 * z * (1.0 + lax.erf(z * np.float32(math.sqrt(0.5))))
    u = act[:, :g]
    v = _layer_norm_rows(act[:, g:], lng_ref[...], lnb_ref[...]).astype(BF16)
    for c in range(r // CHUNK):
        rows = slice(c * CHUNK, (c + 1) * CHUNK)
        for h in range(heads):
            cols = slice(h * CHUNK, (h + 1) * CHUNK)
            sv = jnp.dot(ws_ref[h], v[rows, cols], preferred_element_type=F32) + bsb_ref[:, cols]
            o_ref[rows, cols] = (u[rows, cols] * sv).astype(o_ref.dtype)


def _gmlp_mixer(z, ln_g, ln_b, ws, bs, g):
    t = z.shape[0]
    heads = g // CHUNK
    r = _pick(t, 512)
    bsb = jnp.repeat(bs.T, CHUNK, axis=1)
    vec = pl.BlockSpec((1, g), lambda i: (0, 0))
    return pl.pallas_call(
        functools.partial(_gmlp_kernel, g=g, heads=heads), grid=(t // r,),
        in_specs=[pl.BlockSpec((r, 2 * g), lambda i: (i, 0)), vec, vec,
                  pl.BlockSpec((heads, CHUNK, CHUNK), lambda i: (0, 0, 0)),
                  pl.BlockSpec((CHUNK, g), lambda i: (0, 0))],
        out_specs=pl.BlockSpec((r, g), lambda i: (i, 0)),
        out_shape=jax.ShapeDtypeStruct((t, g), BF16),
        compiler_params=_cparams(("parallel",)), name="gmlp_mixer")(
            z, ln_g.reshape(1, g), ln_b.reshape(1, g), ws.astype(BF16), bsb)


def _halo_specs(r, width, col_block, seq, total):
    per_seq = seq // r
    hb = r // HALO
    last_halo_block = total // HALO - 1

    def main(b, i):
        return (b * per_seq + i, col_block)

    def prev(b, i):
        return (jnp.maximum((b * per_seq + i) * hb - 1, 0), col_block)

    def nxt(b, i):
        return (jnp.minimum((b * per_seq + i + 1) * hb, last_halo_block), col_block)

    return [pl.BlockSpec((r, width), main), pl.BlockSpec((HALO, width), prev),
            pl.BlockSpec((HALO, width), nxt)]


def _conv_kernel(z_ref, zp_ref, zn_ref, w_ref, b_ref, lng_ref, lnb_ref, wsrc_ref,
                 o_ref, wdst_ref, hbuf, cbuf, *, g):
    i = pl.program_id(1)
    _Rounder.run(wsrc_ref, wdst_ref)
    r = z_ref.shape[0]
    rb = cbuf.shape[0]
    slab_rows = rb + 2 * HALO

    def glu(zz):
        return zz[:, :g] * jax.nn.sigmoid(zz[:, g:])

    hbuf[0:HALO, :] = jnp.where(i > 0, glu(zp_ref[...]), 0.0)
    hbuf[HALO:HALO + r, :] = glu(z_ref[...])
    hbuf[HALO + r:, :] = jnp.where(i < pl.num_programs(1) - 1, glu(zn_ref[...]), 0.0)

    def group(q, carry):
        base = pl.multiple_of(q * rb, rb)
        for cb in range(g // LANES):
            cols = slice(cb * LANES, (cb + 1) * LANES)
            slab = hbuf[pl.ds(base, slab_rows), cols]
            acc = jnp.zeros((rb, LANES), F32)
            for b in range(SUBLANES):
                sh = slab if b == 0 else pltpu.roll(slab, slab_rows - b, axis=0)
                for a in range(slab_rows // SUBLANES):
                    k = SUBLANES * a + b - (HALO - CONV_PAD)
                    if 0 <= k < CONV_WIDTH:
                        acc = acc + sh[SUBLANES * a:SUBLANES * a + rb, :] * w_ref[k:k + 1, cols]
            cbuf[:, cols] = acc
        y = _layer_norm_rows(cbuf[...] + b_ref[...], lng_ref[...], lnb_ref[...])
        o_ref[pl.ds(base, rb), :] = (y * jax.nn.sigmoid(y)).astype(o_ref.dtype)
        return carry

    lax.fori_loop(0, r // rb, group, 0)


def _conv_mixer(z, w, b, ln_g, ln_b, g, batch, seq, round_stack, round_layer):
    t = z.shape[0]
    r = _pick(seq, 512)
    rb = _pick(r, CONV_ROWS)
    grid = (batch, seq // r)
    rnd = _Rounder(round_stack, round_layer, grid, lambda bb, i: bb * grid[1] + i)
    vec = pl.BlockSpec((1, g), lambda bb, i: (0, 0))
    return pl.pallas_call(
        functools.partial(_conv_kernel, g=g), grid=grid,
        in_specs=_halo_specs(r, 2 * g, 1, seq, t) + [
            pl.BlockSpec((CONV_WIDTH, g), lambda bb, i: (0, 0)), vec, vec, vec, rnd.in_spec],
        out_specs=[pl.BlockSpec((r, g), lambda bb, i: (bb * (seq // r) + i, 0)), rnd.out_spec],
        out_shape=[jax.ShapeDtypeStruct((t, g), BF16), rnd.out_shape],
        scratch_shapes=[pltpu.VMEM((r + 2 * HALO, g), F32), pltpu.VMEM((rb, g), F32)],
        compiler_params=_cparams(("parallel", "arbitrary")), name="conv_mixer")(
            z, z, z, w, b.reshape(1, g), ln_g.reshape(1, g), ln_b.reshape(1, g), round_stack)


def _pool_kernel(z_ref, zp_ref, zn_ref, pw_ref, sc_ref, o_ref, zbuf, pbuf, *, g, seq):
    i = pl.program_id(1)
    r = z_ref.shape[0]
    pc = g // len(POOL_WINDOWS)
    zbuf[0:HALO, :] = jnp.where(i > 0, zp_ref[...], 0.0)
    zbuf[HALO:HALO + r, :] = z_ref[...]
    zbuf[HALO + r:, :] = jnp.where(i < pl.num_programs(1) - 1, zn_ref[...], 0.0)
    pos = i * r + lax.broadcasted_iota(jnp.int32, (r, 1), 0)
    for gi, win in enumerate(POOL_WINDOWS):
        cols = slice(gi * pc, (gi + 1) * pc)
        half = win // 2
        tot = zbuf[HALO - half:HALO - half + r, cols]
        for d in range(1 - half, half):
            tot = tot + zbuf[HALO + d:HALO + d + r, cols]
        cnt = (jnp.minimum(pos + half, seq) - jnp.maximum(pos - half, 0)).astype(F32)
        pbuf[:, cols] = (tot / cnt - z_ref[:, cols]).astype(BF16)
    for gi in range(len(POOL_WINDOWS)):
        cols = slice(gi * pc, (gi + 1) * pc)
        y = jnp.dot(pbuf[:, cols], pw_ref[gi], preferred_element_type=F32)
        o_ref[:, cols] = (y * sc_ref[:, cols]).astype(o_ref.dtype)


def _pool_mixer(z, pool_w, scale, g, batch, seq):
    t = z.shape[0]
    r = _pick(seq, 512)
    pc = g // len(POOL_WINDOWS)
    return pl.pallas_call(
        functools.partial(_pool_kernel, g=g, seq=seq), grid=(batch, seq // r),
        in_specs=_halo_specs(r, g, 4, seq, t) + [
            pl.BlockSpec((len(POOL_WINDOWS), pc, pc), lambda bb, i: (0, 0, 0)),
            pl.BlockSpec((1, g), lambda bb, i: (0, 0))],
        out_specs=pl.BlockSpec((r, g), lambda bb, i: (bb * (seq // r) + i, 0)),
        out_shape=jax.ShapeDtypeStruct((t, g), BF16),
        scratch_shapes=[pltpu.VMEM((r + 2 * HALO, g), F32), pltpu.VMEM((r, g), BF16)],
        compiler_params=_cparams(("parallel", "arbitrary")), name="pool_mixer")(
            z, z, z, pool_w.astype(BF16), scale.reshape(1, g))


def _qk_prep_kernel(zq_ref, zk_ref, zv_ref, cos_ref, sin_ref, gq_ref, gk_ref, bd_ref,
                    q_ref, kt_ref, v_ref, *, heads):
    lane = lax.broadcasted_iota(jnp.int32, (1, HEAD_W), 1) % DIFF_HEAD_DIM
    first_half = lane < ROT_DIM // 2
    cos = cos_ref[...]
    sin = sin_ref[...]
    bd = bd_ref[...]

    def norm_rope(x, gain):
        x2 = x * x
        hi = x2.astype(BF16)
        lo = (x2 - hi.astype(F32)).astype(BF16)
        ss = (jnp.dot(hi, bd, preferred_element_type=F32)
              + jnp.dot(lo, bd, preferred_element_type=F32))
        xn = x * lax.rsqrt(ss * (1.0 / DIFF_HEAD_DIM) + RMS_EPS) * gain
        partner = jnp.where(first_half,
                            pltpu.roll(xn, HEAD_W - ROT_DIM // 2, axis=1),
                            pltpu.roll(xn, ROT_DIM // 2, axis=1))
        return xn * cos + partner * sin

    for h in range(heads):
        cols = slice(h * HEAD_W, (h + 1) * HEAD_W)
        q = norm_rope(zq_ref[:, cols], gq_ref[...]) * (DIFF_HEAD_DIM ** -0.5 * LOG2_E)
        q_ref[:, cols] = q.astype(q_ref.dtype)
        k = norm_rope(zk_ref[:, cols], gk_ref[...])
        kt_ref[0, h] = k.T.astype(kt_ref.dtype)
    v_ref[...] = zv_ref[...].astype(v_ref.dtype)


def _rope_tables(seq):
    pos = jnp.arange(seq, dtype=F32)
    inv_freq = ROPE_THETA ** (-jnp.arange(0, ROT_DIM, 2, dtype=F32) / ROT_DIM)
    ang = pos[:, None] * inv_freq[None, :]
    cos, sin = jnp.cos(ang), jnp.sin(ang)
    pad = DIFF_HEAD_DIM - ROT_DIM
    cos_c = jnp.concatenate([cos, cos, jnp.ones((seq, pad), F32)], axis=1)
    sin_c = jnp.concatenate([-sin, sin, jnp.zeros((seq, pad), F32)], axis=1)
    return jnp.tile(cos_c, (1, 2)), jnp.tile(sin_c, (1, 2))


def _qk_prep(z, qn_g, kn_g, cos_t, sin_t, g, batch, seq):
    t = z.shape[0]
    heads = g // HEAD_W
    r = _pick(seq, 512)
    per_seq = seq // r
    bd = jnp.kron(jnp.eye(2, dtype=F32), jnp.ones((DIFF_HEAD_DIM, DIFF_HEAD_DIM), F32)).astype(BF16)

    def zcol(c):
        return pl.BlockSpec((r, g), lambda i: (i, c))

    tab = pl.BlockSpec((r, HEAD_W), lambda i: (i % per_seq, 0))
    gain = pl.BlockSpec((1, HEAD_W), lambda i: (0, 0))
    row = pl.BlockSpec((r, g), lambda i: (i, 0))
    return pl.pallas_call(
        functools.partial(_qk_prep_kernel, heads=heads), grid=(t // r,),
        in_specs=[zcol(5), zcol(6), zcol(7), tab, tab, gain, gain,
                  pl.BlockSpec((HEAD_W, HEAD_W), lambda i: (0, 0))],
        out_specs=[row,
                   pl.BlockSpec((1, heads, HEAD_W, r), lambda i: (i // per_seq, 0, 0, i % per_seq)),
                   row],
        out_shape=[jax.ShapeDtypeStruct((t, g), BF16),
                   jax.ShapeDtypeStruct((batch, heads, HEAD_W, seq), BF16),
                   jax.ShapeDtypeStruct((t, g), BF16)],
        compiler_params=_cparams(("parallel",)), name="qk_prep")(
            z, z, z, cos_t, sin_t, jnp.tile(qn_g, 2).reshape(1, HEAD_W),
            jnp.tile(kn_g, 2).reshape(1, HEAD_W), bd)


def _attn_kernel(q_ref, kt_ref, v_ref, lam_ref, sg_ref, o_ref,
                 vext, s_a, s_b, mx_a, mx_b, m_run, acc, *, tq, tk, lam_init):
    seq = v_ref.shape[0]
    d = DIFF_HEAD_DIM
    nk = seq // tk
    n_tiles = seq // tq

    vext[:, :HEAD_W] = v_ref[...]
    vext[:, HEAD_W:] = jnp.ones((seq, HEAD_W), vext.dtype)

    lp = lam_ref[...]
    lam = (jnp.exp(jnp.sum(lp[0:1] * lp[1:2], axis=-1, keepdims=True))
           - jnp.exp(jnp.sum(lp[2:3] * lp[3:4], axis=-1, keepdims=True)) + lam_init)

    def scores(i, j, s_buf, mx_buf):
        qt = q_ref[pl.ds(pl.multiple_of(i * tq, tq), tq), :]
        for c in range(2):
            s = jnp.dot(qt[:, c * d:(c + 1) * d],
                        kt_ref[0, 0, c * d:(c + 1) * d, j * tk:(j + 1) * tk],
                        preferred_element_type=F32)
            s_buf[c] = s
            mx_buf[c] = jnp.broadcast_to(jnp.max(s, axis=-1, keepdims=True), (tq, HEAD_W))

    def accumulate(j, s_buf, mx_buf):
        vj = vext[j * tk:(j + 1) * tk, :]
        for c in range(2):
            m_new = mx_buf[c]
            if j > 0:
                m_new = jnp.maximum(m_run[c], m_new)
                alpha = jnp.exp2(m_run[c] - m_new)
            m_run[c] = m_new
            p = jnp.concatenate(
                [jnp.exp2(s_buf[c, :, kb * HEAD_W:(kb + 1) * HEAD_W] - m_new).astype(BF16)
                 for kb in range(tk // HEAD_W)], axis=1)
            pv = jnp.dot(p, vj, preferred_element_type=F32)
            acc[c] = pv if j == 0 else acc[c] * jnp.concatenate([alpha, alpha], axis=1) + pv

    scores(0, 0, s_a, mx_a)

    def tile(i, carry):
        bufs = ((s_a, mx_a), (s_b, mx_b))
        nxt = jnp.minimum(i + 1, n_tiles - 1)
        for j in range(nk):
            if j + 1 < nk:
                scores(i, j + 1, *bufs[(j + 1) % 2])
            else:
                scores(nxt, 0, *bufs[0])
            accumulate(j, *bufs[j % 2])
        a0 = acc[0]
        a1 = acc[1]
        o = a0[:, :HEAD_W] / a0[:, HEAD_W:] - lam * (a1[:, :HEAD_W] / a1[:, HEAD_W:])
        o = _rms_rows(o, sg_ref[...]) * (1.0 - lam_init)
        o_ref[pl.ds(pl.multiple_of(i * tq, tq), tq), :] = o.astype(o_ref.dtype)
        return carry

    lax.fori_loop(0, n_tiles, tile, 0)


def _diff_attention(q, kt, v, lam_params, subln_g, lam_init, g, batch, seq):
    t = q.shape[0]
    heads = g // HEAD_W
    tq = _pick(seq, ATTN_TQ)
    tk = _pick(seq // 2, ATTN_TK)
    return pl.pallas_call(
        functools.partial(_attn_kernel, tq=tq, tk=tk, lam_init=lam_init),
        grid=(batch, heads),
        in_specs=[pl.BlockSpec((seq, HEAD_W), lambda b, h: (b, h)),
                  pl.BlockSpec((1, 1, HEAD_W, seq), lambda b, h: (b, h, 0, 0)),
                  pl.BlockSpec((seq, HEAD_W), lambda b, h: (b, h)),
                  pl.BlockSpec((4, DIFF_HEAD_DIM), lambda b, h: (0, 0)),
                  pl.BlockSpec((1, HEAD_W), lambda b, h: (0, 0))],
        out_specs=pl.BlockSpec((seq, HEAD_W), lambda b, h: (b, h)),
        out_shape=jax.ShapeDtypeStruct((t, g), BF16),
        scratch_shapes=[pltpu.VMEM((seq, 2 * HEAD_W), BF16),
                        pltpu.VMEM((2, tq, tk), F32), pltpu.VMEM((2, tq, tk), F32),
                        pltpu.VMEM((2, tq, HEAD_W), F32), pltpu.VMEM((2, tq, HEAD_W), F32),
                        pltpu.VMEM((2, tq, HEAD_W), F32), pltpu.VMEM((2, tq, 2 * HEAD_W), F32)],
        compiler_params=_cparams(("parallel", "parallel")), name="diff_attention")(
            q, kt, v, lam_params, subln_g.reshape(1, HEAD_W))


def kernel(x, attn_norm_g, w_in, gmlp_ln_g, gmlp_ln_b, gmlp_ws, gmlp_bs, conv_w, conv_b, conv_ln_g, conv_ln_b, pool_w, pool_scale, q_norm_g, k_norm_g, lambda_q1, lambda_k1, lambda_q2, lambda_k2, subln_g, w_out, mlp_norm_g, w_up, w_down):
    batch, seq, d = x.shape
    depth = w_in.shape[0]
    g = d // 4
    t = batch * seq
    cos_t, sin_t = _rope_tables(seq)

    xr = x.reshape(t, d)
    xg, ss, wb_in = _norm0(xr, attn_norm_g[0], w_in)
    for l in range(depth):
        lam_init = 0.8 - 0.6 * math.exp(-0.3 * l)
        z, wb_up = _scaled_matmul(xg, ss, wb_in, F32, w_up, l, name="in_proj")
        y_a = _gmlp_mixer(z, gmlp_ln_g[l], gmlp_ln_b[l], gmlp_ws[l], gmlp_bs[l], g)
        y_b, wb_out = _conv_mixer(z, conv_w[l], conv_b[l], conv_ln_g[l], conv_ln_b[l], g, batch, seq,
                                  w_out, l)
        y_c = _pool_mixer(z, pool_w[l], pool_scale[l], g, batch, seq)
        q, kt, v = _qk_prep(z, q_norm_g[l], k_norm_g[l], cos_t, sin_t, g, batch, seq)
        lam_params = jnp.stack([lambda_q1[l], lambda_k1[l], lambda_q2[l], lambda_k2[l]])
        y_d = _diff_attention(q, kt, v, lam_params, subln_g[l], lam_init, g, batch, seq)
        xr, xg, ss = _out_proj((y_a, y_b, y_c, y_d), wb_out, xr, mlp_norm_g[l])
        hid, wb_down = _scaled_matmul(xg, ss, wb_up, BF16, w_down, l, relu2=True, name="mlp_up")
        if l + 1 < depth:
            xr, xg, ss, wb_in = _mlp_down(hid, wb_down, xr, attn_norm_g[l + 1], w_in, l + 1)
        else:
            (xr,) = _mlp_down(hid, wb_down, xr)
    return xr.reshape(batch, seq, d)
```

```python
import functools
import math

import jax
import jax.numpy as jnp
import numpy as np
from jax import lax
from jax.experimental import pallas as pl
from jax.experimental.pallas import tpu as pltpu

F32 = jnp.float32
BF16 = jnp.bfloat16

CHUNK = 128
CONV_WIDTH = 31
CONV_PAD = CONV_WIDTH // 2
POOL_WINDOWS = (2, 4, 8, 16)
DIFF_HEAD_DIM = 64
HEAD_W = 2 * DIFF_HEAD_DIM
ROT_DIM = DIFF_HEAD_DIM // 4
ROPE_THETA = 500000.0
LOG2_E = math.log2(math.e)
RMS_EPS = 1e-6
LN_EPS = 1e-5

LANES = 128
SUBLANES = 8

ATTN_TQ = 512
ATTN_TK = 2048
CONV_ROWS = 128
HALO = 16
VMEM_LIMIT = 58 * 1024 * 1024


def _cparams(sem):
    return pltpu.CompilerParams(dimension_semantics=sem, vmem_limit_bytes=VMEM_LIMIT)


def _pick(n, pref):
    t = min(pref, n)
    while n % t:
        t //= 2
    return t


def _rms_rows(x, g):
    ms = jnp.mean(x * x, axis=-1, keepdims=True)
    return x * lax.rsqrt(ms + RMS_EPS) * g


def _lane_partial_sumsq(x):
    x2 = x * x
    part = x2[:, 0:LANES]
    for c in range(1, x.shape[1] // LANES):
        part = part + x2[:, c * LANES:(c + 1) * LANES]
    return part


def _row_rms_scale(ss_ref, inv_d):
    return lax.rsqrt(jnp.sum(ss_ref[...], axis=-1, keepdims=True) * inv_d + RMS_EPS)


class _Rounder:
    def __init__(self, w_stack, layer, grid, step_of):
        _, k, n = w_stack.shape
        steps = math.prod(grid)
        rows = max(2 * SUBLANES, k // steps)
        row_blocks = k // rows
        col_splits = steps // row_blocks
        cols = n // col_splits
        assert rows * row_blocks == k and row_blocks * col_splits == steps, (k, n, steps)
        assert cols * col_splits == n and cols % LANES == 0, (k, n, steps)

        def block_of(*ids):
            s = step_of(*ids)
            return s // col_splits, s % col_splits

        self.in_spec = pl.BlockSpec((1, rows, cols), lambda *ids: (layer, *block_of(*ids)))
        self.out_spec = pl.BlockSpec((rows, cols), lambda *ids: block_of(*ids))
        self.out_shape = jax.ShapeDtypeStruct((k, n), BF16)

    @staticmethod
    def run(src_ref, dst_ref):
        dst_ref[...] = src_ref[0].astype(dst_ref.dtype)


def _norm_inputs_kernel(x_ref, g_ref, *rest):
    xg_ref, ss_ref = rest[-2:] if len(rest) == 2 else rest[1:3]
    x = x_ref[...]
    xg_ref[...] = (x * g_ref[...]).astype(xg_ref.dtype)
    ss_ref[...] = _lane_partial_sumsq(x)
    if len(rest) > 2:
        _Rounder.run(rest[0], rest[3])


def _norm_inputs(x, g, round_stack=None, round_layer=None):
    t, d = x.shape
    r = _pick(t, 256)
    grid = (t // r,)
    row = pl.BlockSpec((r, d), lambda i: (i, 0))
    in_specs = [row, pl.BlockSpec((1, d), lambda i: (0, 0))]
    operands = [x, g.reshape(1, d)]
    out_specs = [row, pl.BlockSpec((r, LANES), lambda i: (i, 0))]
    out_shape = [jax.ShapeDtypeStruct((t, d), BF16), jax.ShapeDtypeStruct((t, LANES), F32)]
    if round_stack is not None:
        rnd = _Rounder(round_stack, round_layer, grid, lambda i: i)
        in_specs.append(rnd.in_spec)
        operands.append(round_stack)
        out_specs.append(rnd.out_spec)
        out_shape.append(rnd.out_shape)
    return pl.pallas_call(
        _norm_inputs_kernel, grid=grid, in_specs=in_specs, out_specs=out_specs,
        out_shape=out_shape, compiler_params=_cparams(("parallel",)), name="norm_inputs")(
            *operands)


def _scaled_mm_kernel(a_ref, w_ref, ss_ref, *rest, relu2, inv_d):
    n_round = (len(rest) - 1) // 2
    wsrc_refs, o_ref, wdst_refs = rest[:n_round], rest[n_round], rest[n_round + 1:]
    acc = jnp.dot(a_ref[...], w_ref[...], preferred_element_type=F32) * _row_rms_scale(ss_ref, inv_d)
    if relu2:
        acc = jnp.square(jnp.maximum(acc, 0.0))
    o_ref[...] = acc.astype(o_ref.dtype)
    for wsrc_ref, wdst_ref in zip(wsrc_refs, wdst_refs):
        _Rounder.run(wsrc_ref, wdst_ref)


def _scaled_matmul(xg, ss, wb, out_dtype, to_round, *, relu2=False, name):
    m, k = xg.shape
    n = wb.shape[1]
    tm, tn = _pick(m, 1024), _pick(n, 1024)
    grid = (m // tm, n // tn)
    rnds = [_Rounder(stack, layer, grid, lambda i, j: i * grid[1] + j) for stack, layer in to_round]
    return pl.pallas_call(
        functools.partial(_scaled_mm_kernel, relu2=relu2, inv_d=1.0 / k), grid=grid,
        in_specs=[pl.BlockSpec((tm, k), lambda i, j: (i, 0)),
                  pl.BlockSpec((k, tn), lambda i, j: (0, j)),
                  pl.BlockSpec((tm, LANES), lambda i, j: (i, 0))] + [r.in_spec for r in rnds],
        out_specs=[pl.BlockSpec((tm, tn), lambda i, j: (i, j))] + [r.out_spec for r in rnds],
        out_shape=[jax.ShapeDtypeStruct((m, n), out_dtype)] + [r.out_shape for r in rnds],
        compiler_params=_cparams(("parallel", "parallel")), name=name)(
            xg, wb, ss, *[stack for stack, _ in to_round])


def _emit_norm_inputs(xn, g_ref, xg_ref, ss_ref, first):
    xg_ref[...] = (xn * g_ref[...]).astype(xg_ref.dtype)
    part = _lane_partial_sumsq(xn)

    @pl.when(first)
    def _():
        ss_ref[...] = part

    @pl.when(jnp.logical_not(first))
    def _():
        ss_ref[...] += part


def _out_proj_kernel(a0_ref, a1_ref, a2_ref, a3_ref, w_ref, x_ref, g_ref, o_ref, xg_ref, ss_ref,
                     *, g):
    acc = x_ref[...]
    for idx, a_ref in enumerate((a0_ref, a1_ref, a2_ref, a3_ref)):
        acc += jnp.dot(a_ref[...], w_ref[idx * g:(idx + 1) * g, :], preferred_element_type=F32)
    o_ref[...] = acc
    _emit_norm_inputs(acc, g_ref, xg_ref, ss_ref, pl.program_id(1) == 0)


def _out_proj(ys, wb, x, gain):
    m, g = ys[0].shape
    n = wb.shape[1]
    tm, tn = _pick(m, 1024), _pick(n, 512)
    a_spec = pl.BlockSpec((tm, g), lambda i, j: (i, 0))
    tile = pl.BlockSpec((tm, tn), lambda i, j: (i, j))
    return pl.pallas_call(
        functools.partial(_out_proj_kernel, g=g), grid=(m // tm, n // tn),
        in_specs=[a_spec, a_spec, a_spec, a_spec, pl.BlockSpec((4 * g, tn), lambda i, j: (0, j)),
                  tile, pl.BlockSpec((1, tn), lambda i, j: (0, j))],
        out_specs=[tile, tile, pl.BlockSpec((tm, LANES), lambda i, j: (i, 0))],
        out_shape=[jax.ShapeDtypeStruct((m, n), F32), jax.ShapeDtypeStruct((m, n), BF16),
                   jax.ShapeDtypeStruct((m, LANES), F32)],
        compiler_params=_cparams(("parallel", "arbitrary")), name="out_proj")(
            *ys, wb, x, gain.reshape(1, n))


def _mlp_down_kernel(a_ref, w_ref, x_ref, o_ref):
    @pl.when(pl.program_id(2) == 0)
    def _():
        o_ref[...] = x_ref[...]

    o_ref[...] += jnp.dot(a_ref[...], w_ref[...], preferred_element_type=F32)


def _mlp_down(a, wb, x):
    m, k = a.shape
    n = wb.shape[1]
    tm, tn, tk = _pick(m, 1024), _pick(n, 1024), _pick(k, 4096)
    tile = pl.BlockSpec((tm, tn), lambda i, j, kk: (i, j))
    return pl.pallas_call(
        _mlp_down_kernel, grid=(m // tm, n // tn, k // tk),
        in_specs=[pl.BlockSpec((tm, tk), lambda i, j, kk: (i, kk)),
                  pl.BlockSpec((tk, tn), lambda i, j, kk: (kk, j)), tile],
        out_specs=tile, out_shape=jax.ShapeDtypeStruct((m, n), F32),
        compiler_params=_cparams(("parallel", "parallel", "arbitrary")), name="mlp_down")(a, wb, x)


def _layer_norm_rows(x, g, b):
    mu = jnp.mean(x, axis=-1, keepdims=True)
    xc = x - mu
    var = jnp.mean(xc * xc, axis=-1, keepdims=True)
    return xc * lax.rsqrt(var + LN_EPS) * g + b


def _gmlp_kernel(z_ref, lng_ref, lnb_ref, ws_ref, bsb_ref, o_ref, *, g, heads):
    r = z_ref.shape[0]
    z = z_ref[...]
    act = 0.5 * z * (1.0 + lax.erf(z * np.float32(math.sqrt(0.5))))
    u = act[:, :g]
    v = _layer_norm_rows(act[:, g:], lng_ref[...], lnb_ref[...]).astype(BF16)
    for c in range(r // CHUNK):
        rows = slice(c * CHUNK, (c + 1) * CHUNK)
        for h in range(heads):
            cols = slice(h * CHUNK, (h + 1) * CHUNK)
            sv = jnp.dot(ws_ref[h], v[rows, cols], preferred_element_type=F32) + bsb_ref[:, cols]
            o_ref[rows, cols] = (u[rows, cols] * sv).astype(o_ref.dtype)


def _gmlp_mixer(z, ln_g, ln_b, ws, bs, g):
    t = z.shape[0]
    heads = g // CHUNK
    r = _pick(t, 512)
    bsb = jnp.repeat(bs.T, CHUNK, axis=1)
    vec = pl.BlockSpec((1, g), lambda i: (0, 0))
    return pl.pallas_call(
        functools.partial(_gmlp_kernel, g=g, heads=heads), grid=(t // r,),
        in_specs=[pl.BlockSpec((r, 2 * g), lambda i: (i, 0)), vec, vec,
                  pl.BlockSpec((heads, CHUNK, CHUNK), lambda i: (0, 0, 0)),
                  pl.BlockSpec((CHUNK, g), lambda i: (0, 0))],
        out_specs=pl.BlockSpec((r, g), lambda i: (i, 0)),
        out_shape=jax.ShapeDtypeStruct((t, g), BF16),
        compiler_params=_cparams(("parallel",)), name="gmlp_mixer")(
            z, ln_g.reshape(1, g), ln_b.reshape(1, g), ws.astype(BF16), bsb)


def _halo_specs(r, width, col_block, seq, total):
    per_seq = seq // r
    hb = r // HALO
    last_halo_block = total // HALO - 1

    def main(b, i):
        return (b * per_seq + i, col_block)

    def prev(b, i):
        return (jnp.maximum((b * per_seq + i) * hb - 1, 0), col_block)

    def nxt(b, i):
        return (jnp.minimum((b * per_seq + i + 1) * hb, last_halo_block), col_block)

    return [pl.BlockSpec((r, width), main), pl.BlockSpec((HALO, width), prev),
            pl.BlockSpec((HALO, width), nxt)]


def _conv_kernel(z_ref, zp_ref, zn_ref, w_ref, b_ref, lng_ref, lnb_ref, wsrc_ref,
                 o_ref, wdst_ref, hbuf, cbuf, *, g):
    i = pl.program_id(1)
    _Rounder.run(wsrc_ref, wdst_ref)
    r = z_ref.shape[0]
    rb = cbuf.shape[0]
    slab_rows = rb + 2 * HALO

    def glu(zz):
        return zz[:, :g] * jax.nn.sigmoid(zz[:, g:])

    hbuf[0:HALO, :] = jnp.where(i > 0, glu(zp_ref[...]), 0.0)
    hbuf[HALO:HALO + r, :] = glu(z_ref[...])
    hbuf[HALO + r:, :] = jnp.where(i < pl.num_programs(1) - 1, glu(zn_ref[...]), 0.0)

    def group(q, carry):
        base = pl.multiple_of(q * rb, rb)
        for cb in range(g // LANES):
            cols = slice(cb * LANES, (cb + 1) * LANES)
            slab = hbuf[pl.ds(base, slab_rows), cols]
            acc = jnp.zeros((rb, LANES), F32)
            for b in range(SUBLANES):
                sh = slab if b == 0 else pltpu.roll(slab, slab_rows - b, axis=0)
                for a in range(slab_rows // SUBLANES):
                    k = SUBLANES * a + b - (HALO - CONV_PAD)
                    if 0 <= k < CONV_WIDTH:
                        acc = acc + sh[SUBLANES * a:SUBLANES * a + rb, :] * w_ref[k:k + 1, cols]
            cbuf[:, cols] = acc
        y = _layer_norm_rows(cbuf[...] + b_ref[...], lng_ref[...], lnb_ref[...])
        o_ref[pl.ds(base, rb), :] = (y * jax.nn.sigmoid(y)).astype(o_ref.dtype)
        return carry

    lax.fori_loop(0, r // rb, group, 0)


def _conv_mixer(z, w, b, ln_g, ln_b, g, batch, seq, round_stack, round_layer):
    t = z.shape[0]
    r = _pick(seq, 512)
    rb = _pick(r, CONV_ROWS)
    grid = (batch, seq // r)
    rnd = _Rounder(round_stack, round_layer, grid, lambda bb, i: bb * grid[1] + i)
    vec = pl.BlockSpec((1, g), lambda bb, i: (0, 0))
    return pl.pallas_call(
        functools.partial(_conv_kernel, g=g), grid=grid,
        in_specs=_halo_specs(r, 2 * g, 1, seq, t) + [
            pl.BlockSpec((CONV_WIDTH, g), lambda bb, i: (0, 0)), vec, vec, vec, rnd.in_spec],
        out_specs=[pl.BlockSpec((r, g), lambda bb, i: (bb * (seq // r) + i, 0)), rnd.out_spec],
        out_shape=[jax.ShapeDtypeStruct((t, g), BF16), rnd.out_shape],
        scratch_shapes=[pltpu.VMEM((r + 2 * HALO, g), F32), pltpu.VMEM((rb, g), F32)],
        compiler_params=_cparams(("parallel", "arbitrary")), name="conv_mixer")(
            z, z, z, w, b.reshape(1, g), ln_g.reshape(1, g), ln_b.reshape(1, g), round_stack)


def _pool_kernel(z_ref, zp_ref, zn_ref, pw_ref, sc_ref, o_ref, zbuf, pbuf, *, g, seq):
    i = pl.program_id(1)
    r = z_ref.shape[0]
    pc = g // len(POOL_WINDOWS)
    zbuf[0:HALO, :] = jnp.where(i > 0, zp_ref[...], 0.0)
    zbuf[HALO:HALO + r, :] = z_ref[...]
    zbuf[HALO + r:, :] = jnp.where(i < pl.num_programs(1) - 1, zn_ref[...], 0.0)
    pos = i * r + lax.broadcasted_iota(jnp.int32, (r, 1), 0)
    for gi, win in enumerate(POOL_WINDOWS):
        cols = slice(gi * pc, (gi + 1) * pc)
        half = win // 2
        tot = zbuf[HALO - half:HALO - half + r, cols]
        for d in range(1 - half, half):
            tot = tot + zbuf[HALO + d:HALO + d + r, cols]
        cnt = (jnp.minimum(pos + half, seq) - jnp.maximum(pos - half, 0)).astype(F32)
        pbuf[:, cols] = (tot / cnt - z_ref[:, cols]).astype(BF16)
    for gi in range(len(POOL_WINDOWS)):
        cols = slice(gi * pc, (gi + 1) * pc)
        y = jnp.dot(pbuf[:, cols], pw_ref[gi], preferred_element_type=F32)
        o_ref[:, cols] = (y * sc_ref[:, cols]).astype(o_ref.dtype)


def _pool_mixer(z, pool_w, scale, g, batch, seq):
    t = z.shape[0]
    r = _pick(seq, 512)
    pc = g // len(POOL_WINDOWS)
    return pl.pallas_call(
        functools.partial(_pool_kernel, g=g, seq=seq), grid=(batch, seq // r),
        in_specs=_halo_specs(r, g, 4, seq, t) + [
            pl.BlockSpec((len(POOL_WINDOWS), pc, pc), lambda bb, i: (0, 0, 0)),
            pl.BlockSpec((1, g), lambda bb, i: (0, 0))],
        out_specs=pl.BlockSpec((r, g), lambda bb, i: (bb * (seq // r) + i, 0)),
        out_shape=jax.ShapeDtypeStruct((t, g), BF16),
        scratch_shapes=[pltpu.VMEM((r + 2 * HALO, g), F32), pltpu.VMEM((r, g), BF16)],
        compiler_params=_cparams(("parallel", "arbitrary")), name="pool_mixer")(
            z, z, z, pool_w.astype(BF16), scale.reshape(1, g))


def _qk_prep_kernel(zq_ref, zk_ref, zv_ref, cos_ref, sin_ref, gq_ref, gk_ref, bd_ref,
                    q_ref, kt_ref, v_ref, *, heads):
    lane = lax.broadcasted_iota(jnp.int32, (1, HEAD_W), 1) % DIFF_HEAD_DIM
    first_half = lane < ROT_DIM // 2
    cos = cos_ref[...]
    sin = sin_ref[...]
    bd = bd_ref[...]

    def norm_rope(x, gain):
        x2 = x * x
        hi = x2.astype(BF16)
        lo = (x2 - hi.astype(F32)).astype(BF16)
        ss = (jnp.dot(hi, bd, preferred_element_type=F32)
              + jnp.dot(lo, bd, preferred_element_type=F32))
        xn = x * lax.rsqrt(ss * (1.0 / DIFF_HEAD_DIM) + RMS_EPS) * gain
        partner = jnp.where(first_half,
                            pltpu.roll(xn, HEAD_W - ROT_DIM // 2, axis=1),
                            pltpu.roll(xn, ROT_DIM // 2, axis=1))
        return xn * cos + partner * sin

    for h in range(heads):
        cols = slice(h * HEAD_W, (h + 1) * HEAD_W)
        q = norm_rope(zq_ref[:, cols], gq_ref[...]) * (DIFF_HEAD_DIM ** -0.5 * LOG2_E)
        q_ref[:, cols] = q.astype(q_ref.dtype)
        k = norm_rope(zk_ref[:, cols], gk_ref[...])
        kt_ref[0, h] = k.T.astype(kt_ref.dtype)
    v_ref[...] = zv_ref[...].astype(v_ref.dtype)


def _rope_tables(seq):
    pos = jnp.arange(seq, dtype=F32)
    inv_freq = ROPE_THETA ** (-jnp.arange(0, ROT_DIM, 2, dtype=F32) / ROT_DIM)
    ang = pos[:, None] * inv_freq[None, :]
    cos, sin = jnp.cos(ang), jnp.sin(ang)
    pad = DIFF_HEAD_DIM - ROT_DIM
    cos_c = jnp.concatenate([cos, cos, jnp.ones((seq, pad), F32)], axis=1)
    sin_c = jnp.concatenate([-sin, sin, jnp.zeros((seq, pad), F32)], axis=1)
    return jnp.tile(cos_c, (1, 2)), jnp.tile(sin_c, (1, 2))


def _qk_prep(z, qn_g, kn_g, cos_t, sin_t, g, batch, seq):
    t = z.shape[0]
    heads = g // HEAD_W
    r = _pick(seq, 512)
    per_seq = seq // r
    bd = jnp.kron(jnp.eye(2, dtype=F32), jnp.ones((DIFF_HEAD_DIM, DIFF_HEAD_DIM), F32)).astype(BF16)

    def zcol(c):
        return pl.BlockSpec((r, g), lambda i: (i, c))

    tab = pl.BlockSpec((r, HEAD_W), lambda i: (i % per_seq, 0))
    gain = pl.BlockSpec((1, HEAD_W), lambda i: (0, 0))
    row = pl.BlockSpec((r, g), lambda i: (i, 0))
    return pl.pallas_call(
        functools.partial(_qk_prep_kernel, heads=heads), grid=(t // r,),
        in_specs=[zcol(5), zcol(6), zcol(7), tab, tab, gain, gain,
                  pl.BlockSpec((HEAD_W, HEAD_W), lambda i: (0, 0))],
        out_specs=[row,
                   pl.BlockSpec((1, heads, HEAD_W, r), lambda i: (i // per_seq, 0, 0, i % per_seq)),
                   row],
        out_shape=[jax.ShapeDtypeStruct((t, g), BF16),
                   jax.ShapeDtypeStruct((batch, heads, HEAD_W, seq), BF16),
                   jax.ShapeDtypeStruct((t, g), BF16)],
        compiler_params=_cparams(("parallel",)), name="qk_prep")(
            z, z, z, cos_t, sin_t, jnp.tile(qn_g, 2).reshape(1, HEAD_W),
            jnp.tile(kn_g, 2).reshape(1, HEAD_W), bd)


def _attn_kernel(q_ref, kt_ref, v_ref, lam_ref, sg_ref, o_ref,
                 vext, s_a, s_b, mx_a, mx_b, m_run, acc, *, tq, tk, lam_init):
    seq = v_ref.shape[0]
    d = DIFF_HEAD_DIM
    nk = seq // tk
    n_tiles = seq // tq

    vext[:, :HEAD_W] = v_ref[...]
    vext[:, HEAD_W:] = jnp.ones((seq, HEAD_W), vext.dtype)

    lp = lam_ref[...]
    lam = (jnp.exp(jnp.sum(lp[0:1] * lp[1:2], axis=-1, keepdims=True))
           - jnp.exp(jnp.sum(lp[2:3] * lp[3:4], axis=-1, keepdims=True)) + lam_init)

    def scores(i, j, s_buf, mx_buf):
        qt = q_ref[pl.ds(pl.multiple_of(i * tq, tq), tq), :]
        for c in range(2):
            s = jnp.dot(qt[:, c * d:(c + 1) * d],
                        kt_ref[0, 0, c * d:(c + 1) * d, j * tk:(j + 1) * tk],
                        preferred_element_type=F32)
            s_buf[c] = s
            mx_buf[c] = jnp.broadcast_to(jnp.max(s, axis=-1, keepdims=True), (tq, HEAD_W))

    def accumulate(j, s_buf, mx_buf):
        vj = vext[j * tk:(j + 1) * tk, :]
        for c in range(2):
            m_new = mx_buf[c]
            if j > 0:
                m_new = jnp.maximum(m_run[c], m_new)
                alpha = jnp.exp2(m_run[c] - m_new)
            m_run[c] = m_new
            p = jnp.concatenate(
                [jnp.exp2(s_buf[c, :, kb * HEAD_W:(kb + 1) * HEAD_W] - m_new).astype(BF16)
                 for kb in range(tk // HEAD_W)], axis=1)
            pv = jnp.dot(p, vj, preferred_element_type=F32)
            acc[c] = pv if j == 0 else acc[c] * jnp.concatenate([alpha, alpha], axis=1) + pv

    scores(0, 0, s_a, mx_a)

    def tile(i, carry):
        bufs = ((s_a, mx_a), (s_b, mx_b))
        nxt = jnp.minimum(i + 1, n_tiles - 1)
        for j in range(nk):
            if j + 1 < nk:
                scores(i, j + 1, *bufs[(j + 1) % 2])
            else:
                scores(nxt, 0, *bufs[0])
            accumulate(j, *bufs[j % 2])
        a0 = acc[0]
        a1 = acc[1]
        o = a0[:, :HEAD_W] / a0[:, HEAD_W:] - lam * (a1[:, :HEAD_W] / a1[:, HEAD_W:])
        o = _rms_rows(o, sg_ref[...]) * (1.0 - lam_init)
        o_ref[pl.ds(pl.multiple_of(i * tq, tq), tq), :] = o.astype(o_ref.dtype)
        return carry

    lax.fori_loop(0, n_tiles, tile, 0)


def _diff_attention(q, kt, v, lam_params, subln_g, lam_init, g, batch, seq):
    t = q.shape[0]
    heads = g // HEAD_W
    tq = _pick(seq, ATTN_TQ)
    tk = _pick(seq // 2, ATTN_TK)
    return pl.pallas_call(
        functools.partial(_attn_kernel, tq=tq, tk=tk, lam_init=lam_init),
        grid=(batch, heads),
        in_specs=[pl.BlockSpec((seq, HEAD_W), lambda b, h: (b, h)),
                  pl.BlockSpec((1, 1, HEAD_W, seq), lambda b, h: (b, h, 0, 0)),
                  pl.BlockSpec((seq, HEAD_W), lambda b, h: (b, h)),
                  pl.BlockSpec((4, DIFF_HEAD_DIM), lambda b, h: (0, 0)),
                  pl.BlockSpec((1, HEAD_W), lambda b, h: (0, 0))],
        out_specs=pl.BlockSpec((seq, HEAD_W), lambda b, h: (b, h)),
        out_shape=jax.ShapeDtypeStruct((t, g), BF16),
        scratch_shapes=[pltpu.VMEM((seq, 2 * HEAD_W), BF16),
                        pltpu.VMEM((2, tq, tk), F32), pltpu.VMEM((2, tq, tk), F32),
                        pltpu.VMEM((2, tq, HEAD_W), F32), pltpu.VMEM((2, tq, HEAD_W), F32),
                        pltpu.VMEM((2, tq, HEAD_W), F32), pltpu.VMEM((2, tq, 2 * HEAD_W), F32)],
        compiler_params=_cparams(("parallel", "parallel")), name="diff_attention")(
            q, kt, v, lam_params, subln_g.reshape(1, HEAD_W))


def kernel(x, attn_norm_g, w_in, gmlp_ln_g, gmlp_ln_b, gmlp_ws, gmlp_bs, conv_w, conv_b, conv_ln_g, conv_ln_b, pool_w, pool_scale, q_norm_g, k_norm_g, lambda_q1, lambda_k1, lambda_q2, lambda_k2, subln_g, w_out, mlp_norm_g, w_up, w_down):
    batch, seq, d = x.shape
    depth = w_in.shape[0]
    g = d // 4
    t = batch * seq
    cos_t, sin_t = _rope_tables(seq)

    xr = x.reshape(t, d)
    xg, ss, wb_in = _norm_inputs(xr, attn_norm_g[0], w_in, 0)
    for l in range(depth):
        lam_init = 0.8 - 0.6 * math.exp(-0.3 * l)
        z, wb_up = _scaled_matmul(xg, ss, wb_in, F32, [(w_up, l)], name="in_proj")
        y_a = _gmlp_mixer(z, gmlp_ln_g[l], gmlp_ln_b[l], gmlp_ws[l], gmlp_bs[l], g)
        y_b, wb_out = _conv_mixer(z, conv_w[l], conv_b[l], conv_ln_g[l], conv_ln_b[l], g, batch, seq,
                                  w_out, l)
        y_c = _pool_mixer(z, pool_w[l], pool_scale[l], g, batch, seq)
        q, kt, v = _qk_prep(z, q_norm_g[l], k_norm_g[l], cos_t, sin_t, g, batch, seq)
        lam_params = jnp.stack([lambda_q1[l], lambda_k1[l], lambda_q2[l], lambda_k2[l]])
        y_d = _diff_attention(q, kt, v, lam_params, subln_g[l], lam_init, g, batch, seq)
        xr, xg, ss = _out_proj((y_a, y_b, y_c, y_d), wb_out, xr, mlp_norm_g[l])
        more = l + 1 < depth
        to_round = [(w_down, l)] + ([(w_in, l + 1)] if more else [])
        hid, wb_down, *wb_next = _scaled_matmul(xg, ss, wb_up, BF16, to_round, relu2=True,
                                                name="mlp_up")
        xr = _mlp_down(hid, wb_down, xr)
        if more:
            (wb_in,) = wb_next
            xg, ss = _norm_inputs(xr, attn_norm_g[l + 1])
    return xr.reshape(batch, seq, d)
```
